```python
import jax, jax.numpy as jnp
from jax import lax
import numpy as np

D_MODEL = 1024
BATCH = 4
SEQ = 4096
DEPTH = 1

MIX_WIDTH = D_MODEL
HG_WIDTH = MIX_WIDTH // 2
HG_HEAD_DIM = 128
HG_HEADS = HG_WIDTH // HG_HEAD_DIM
HG_CHUNK = 64
RW_WIDTH = MIX_WIDTH - HG_WIDTH
RW_HEAD_DIM = 64
RW_HEADS = RW_WIDTH // RW_HEAD_DIM
RW_DECAY_LORA = 64
RW_AAA_LORA = 64
RW_GATE_LORA = 128
RW_COLS = 3 * RW_WIDTH + RW_DECAY_LORA + RW_AAA_LORA + RW_GATE_LORA
HG_COLS = 4 * HG_WIDTH
IN_COLS = HG_COLS + RW_COLS
D_FF = 2816
CONV_WIDTH = 3
NORM_EPS = 1e-6
RW_GN_EPS = 64e-5
L2_EPS = 1e-12

kernel_name = "hybrid_hgrn2_rwkv7_convffn"


def _rmsnorm(x, w):
    xf = x.astype(jnp.float32)
    y = xf * lax.rsqrt(jnp.mean(xf * xf, axis=-1, keepdims=True) + NORM_EPS)
    return (y * w.astype(jnp.float32)).astype(x.dtype)


def _token_shift(z):
    return jnp.pad(z, ((0, 0), (1, 0), (0, 0)))[:, :-1]


def _hgrn2_chunk_scan(q, k, v, logf):
    B, T, H, K = q.shape
    V = v.shape[-1]
    C = HG_CHUNK
    N = T // C

    def to_chunks(z):
        return z.reshape(B, N, C, H, z.shape[-1]).transpose(1, 0, 3, 2, 4)

    qc, kc, vc, gc = to_chunks(q), to_chunks(k), to_chunks(v), to_chunks(logf)
    causal = jnp.tril(jnp.ones((C, C), dtype=bool))[:, :, None]

    def step(S, inp):
        qb, kb, vb, gb = inp
        b = jnp.cumsum(gb, axis=2)
        diff = b[:, :, :, None, :] - b[:, :, None, :, :]
        dec = jnp.exp(jnp.where(causal, diff, -jnp.inf))
        A = jnp.einsum('bhtk,bhsk,bhtsk->bhts', qb, kb, dec)
        o = jnp.einsum('bhts,bhsv->bhtv', A, vb) + jnp.einsum('bhtk,bhkv->bhtv', qb * jnp.exp(b), S)
        b_last = b[:, :, -1:, :]
        S = jnp.exp(b_last[:, :, 0, :])[..., None] * S + jnp.einsum(
            'bhsk,bhsv->bhkv', kb * jnp.exp(b_last - b), vb)
        return S, o

    S0 = jnp.zeros((B, H, K, V), jnp.float32)
    _, o = lax.scan(step, S0, (qc, kc, vc, gc))
    return o.transpose(1, 0, 3, 2, 4).reshape(B, T, H, V)


def _hgrn2_mixer(q_raw, f_raw, i_raw, g_raw, lb, norm_w):
    B, T, _ = q_raw.shape
    f32 = jnp.float32

    def heads(z):
        return z.reshape(B, T, HG_HEADS, HG_HEAD_DIM)

    f = lb + (1.0 - lb) * jax.nn.sigmoid(f_raw.astype(f32))
    q = heads(jax.nn.silu(q_raw.astype(f32))) * (HG_HEAD_DIM ** -0.5)
    k = heads(1.0 - f)
    logf = heads(jnp.log(f))
    v = heads(i_raw.astype(f32))
    o = _hgrn2_chunk_scan(q, k, v, logf)
    o = o * lax.rsqrt(jnp.mean(o * o, axis=-1, keepdims=True) + NORM_EPS)
    o = o.reshape(B, T, HG_WIDTH) * norm_w.astype(f32) * jax.nn.silu(g_raw.astype(f32))
    return o.astype(q_raw.dtype)


def _rwkv7_scan(r, w, k, v, a, b):
    B, T, H, N = r.shape

    def step(S, inp):
        rt, wt, kt, vt, at, bt = inp
        sa = jnp.einsum('bhvk,bhk->bhv', S, at)
        S = S * wt[:, :, None, :] + sa[..., None] * bt[:, :, None, :] + vt[..., None] * kt[:, :, None, :]
        y = jnp.einsum('bhvk,bhk->bhv', S, rt)
        return S, y

    xs = tuple(z.transpose(1, 0, 2, 3) for z in (r, w, k, v, a, b))
    S0 = jnp.zeros((B, H, N, N), jnp.float32)
    _, y = lax.scan(step, S0, xs)
    return y.transpose(1, 0, 2, 3)


def _rwkv7_mixer(r, k, v, wd, ad, gd, w0, w2, a0, a2, g2, k_k, k_a, r_k, ln_w, ln_b):
    B, T, _ = r.shape
    out_dtype = r.dtype
    f32 = jnp.float32

    def heads(z):
        return z.reshape(B, T, RW_HEADS, RW_HEAD_DIM)

    r, k, v = r.astype(f32), k.astype(f32), v.astype(f32)
    w = -jax.nn.softplus(-(w0.astype(f32) + jnp.tanh(wd.astype(f32)) @ w2.astype(f32))) - 0.5
    decay = jnp.exp(-jnp.exp(w))
    a = jax.nn.sigmoid(a0.astype(f32) + ad.astype(f32) @ a2.astype(f32))
    g = jax.nn.sigmoid(gd.astype(f32)) @ g2.astype(f32)
    kk = heads(k * k_k.astype(f32))
    kk = kk / jnp.maximum(jnp.sqrt(jnp.sum(kk * kk, axis=-1, keepdims=True)), L2_EPS)
    k = k * (1.0 + (a - 1.0) * k_a.astype(f32))
    y = _rwkv7_scan(heads(r), heads(decay), heads(k), heads(v), -kk, kk * heads(a))
    mu = jnp.mean(y, axis=-1, keepdims=True)
    var = jnp.mean(jnp.square(y - mu), axis=-1, keepdims=True)
    y = ((y - mu) * lax.rsqrt(var + RW_GN_EPS)).reshape(B, T, RW_WIDTH)
    y = y * ln_w.astype(f32) + ln_b.astype(f32)
    bonus = jnp.sum(heads(r * k * r_k.astype(f32)), axis=-1, keepdims=True) * heads(v)
    y = y + bonus.reshape(B, T, RW_WIDTH)
    return (y * g).astype(out_dtype)


def _conv_ffn(h, w_up, conv_w, conv_b, w_down):
    u = h @ w_up
    C = u.shape[-1]
    u = lax.conv_general_dilated(
        u, conv_w[:, None, :], window_strides=(1,), padding=[(CONV_WIDTH - 1, 0)],
        dimension_numbers=('NWC', 'WIO', 'NWC'), feature_group_count=C) + conv_b
    gate, val = jnp.split(u, 2, axis=-1)
    return (jax.nn.silu(gate) * val) @ w_down


def setup_inputs(seed: int = 0) -> dict:
    key = jax.random.key(seed)
    ks = jax.random.split(key, 24)
    f32 = jnp.float32
    nrm = lambda k, s: jax.random.normal(k, s, f32)
    L = DEPTH
    return {
        "x": nrm(ks[0], (BATCH, SEQ, D_MODEL)),
        "norm1_w": 1.0 + 0.02 * nrm(ks[1], (L, D_MODEL)),
        "w_in": nrm(ks[2], (L, D_MODEL, IN_COLS)) * D_MODEL ** -0.5,
        "hg_lb_logits": 0.1 * nrm(ks[3], (L + 1, HG_WIDTH)),
        "hg_norm_w": 1.0 + 0.02 * nrm(ks[4], (L, HG_WIDTH)),
        "rw_shift_mu": jax.random.uniform(ks[5], (L, RW_COLS), f32),
        "rw_w0": jax.random.uniform(ks[6], (L, RW_WIDTH), f32, -6.5, -1.5),
        "rw_w2": 0.1 * nrm(ks[7], (L, RW_DECAY_LORA, RW_WIDTH)) * RW_DECAY_LORA ** -0.5,
        "rw_a0": 0.1 * nrm(ks[8], (L, RW_WIDTH)),
        "rw_a2": 0.1 * nrm(ks[9], (L, RW_AAA_LORA, RW_WIDTH)) * RW_AAA_LORA ** -0.5,
        "rw_g2": nrm(ks[10], (L, RW_GATE_LORA, RW_WIDTH)) * RW_GATE_LORA ** -0.5,
        "rw_k_k": 0.85 + 0.02 * nrm(ks[11], (L, RW_WIDTH)),
        "rw_k_a": 1.0 + 0.02 * nrm(ks[12], (L, RW_WIDTH)),
        "rw_r_k": 0.1 * nrm(ks[13], (L, RW_WIDTH)),
        "rw_ln_w": 1.0 + 0.02 * nrm(ks[14], (L, RW_WIDTH)),
        "rw_ln_b": 0.02 * nrm(ks[15], (L, RW_WIDTH)),
        "w_out": nrm(ks[16], (L, MIX_WIDTH, D_MODEL)) * MIX_WIDTH ** -0.5,
        "norm2_w": 1.0 + 0.02 * nrm(ks[17], (L, D_MODEL)),
        "w_up": nrm(ks[18], (L, D_MODEL, 2 * D_FF)) * D_MODEL ** -0.5,
        "conv_w": nrm(ks[19], (L, CONV_WIDTH, 2 * D_FF)) * CONV_WIDTH ** -0.5,
        "conv_b": 0.02 * nrm(ks[20], (L, 2 * D_FF)),
        "w_down": nrm(ks[21], (L, D_FF, D_MODEL)) * D_FF ** -0.5,
        "final_norm_w": 1.0 + 0.02 * nrm(ks[22], (D_MODEL,)),
    }


def reference(x, norm1_w, w_in, hg_lb_logits, hg_norm_w, rw_shift_mu, rw_w0, rw_w2, rw_a0, rw_a2,
              rw_g2, rw_k_k, rw_k_a, rw_r_k, rw_ln_w, rw_ln_b, w_out, norm2_w, w_up, conv_w,
              conv_b, w_down, final_norm_w):
    lower_bounds = jnp.cumsum(jax.nn.softmax(hg_lb_logits.astype(jnp.float32), axis=0), axis=0)
    rw_split = np.cumsum([RW_WIDTH, RW_WIDTH, RW_WIDTH, RW_DECAY_LORA, RW_AAA_LORA]).tolist()
    for l in range(DEPTH):
        h = _rmsnorm(x, norm1_w[l])
        proj = h @ w_in[l]
        hg_p, rw_p = proj[..., :HG_COLS], proj[..., HG_COLS:]
        q_raw, f_raw, i_raw, g_raw = jnp.split(hg_p, 4, axis=-1)
        o_hg = _hgrn2_mixer(q_raw, f_raw, i_raw, g_raw, lower_bounds[l], hg_norm_w[l])
        rw_p = rw_p + (_token_shift(rw_p) - rw_p) * rw_shift_mu[l]
        r, k, v, wd, ad, gd = jnp.split(rw_p, rw_split, axis=-1)
        o_rw = _rwkv7_mixer(r, k, v, wd, ad, gd, rw_w0[l], rw_w2[l], rw_a0[l], rw_a2[l], rw_g2[l],
                            rw_k_k[l], rw_k_a[l], rw_r_k[l], rw_ln_w[l], rw_ln_b[l])
        x = x + jnp.concatenate([o_hg, o_rw], axis=-1) @ w_out[l]
        x = x + _conv_ffn(_rmsnorm(x, norm2_w[l]), w_up[l], conv_w[l], conv_b[l], w_down[l])
    return _rmsnorm(x, final_norm_w)
```

```python
import jax
import jax.numpy as jnp
from jax import lax
from jax.experimental import pallas as pl
from jax.experimental.pallas import tpu as pltpu

F32 = jnp.float32
BF16 = jnp.bfloat16

D_MODEL = 1024
HG_WIDTH = 512
HG_HEAD_DIM = 128
HG_HEADS = HG_WIDTH // HG_HEAD_DIM
RW_WIDTH = 512
RW_HEAD_DIM = 64
RW_DECAY_LORA = 64
RW_AAA_LORA = 64
RW_GATE_LORA = 128
RW_COLS = 3 * RW_WIDTH + RW_DECAY_LORA + RW_AAA_LORA + RW_GATE_LORA
HG_COLS = 4 * HG_WIDTH
IN_COLS = HG_COLS + RW_COLS
D_FF = 2816
NORM_EPS = 1e-6
RW_GN_EPS = 64e-5
L2_EPS = 1e-12

LANES = 128
CHUNK = 64
RW_PAIRS = RW_WIDTH // LANES
VMEM_LIMIT_BYTES = 56 * 1024 * 1024

ROW_TILE = 512
SCAN_TILE = 256
FFN_COL_TILE = 256


def _bf(x):
    return x.astype(BF16)


def _dot(a, b):
    return jnp.dot(_bf(a), _bf(b), preferred_element_type=F32)


def _dot_nt(a, b):
    return lax.dot_general(_bf(a), _bf(b), (((1,), (1,)), ((), ())), preferred_element_type=F32)


def _dot_tn(a, b):
    return jnp.dot(_bf(a.T), _bf(b), preferred_element_type=F32)


def _split3(x):
    h1 = _bf(x)
    r1 = x - h1.astype(F32)
    h2 = _bf(r1)
    r2 = r1 - h2.astype(F32)
    return h1, h2, _bf(r2)


def _dot_exact_rhs(m01, x):
    m = _bf(m01)
    h1, h2, h3 = _split3(x)
    return (jnp.dot(m, h1, preferred_element_type=F32)
            + jnp.dot(m, h2, preferred_element_type=F32)
            + jnp.dot(m, h3, preferred_element_type=F32))


def _dot_exact_lhs(x, m01):
    m = _bf(m01)
    h1, h2, h3 = _split3(x)
    return (jnp.dot(h1, m, preferred_element_type=F32)
            + jnp.dot(h2, m, preferred_element_type=F32)
            + jnp.dot(h3, m, preferred_element_type=F32))


def _iota2(shape, dim):
    return lax.broadcasted_iota(jnp.int32, shape, dim)


def _chunk_tril(n):
    r = _iota2((n, n), 0)
    c = _iota2((n, n), 1)
    return jnp.where((r // CHUNK == c // CHUNK) & (c <= r), 1.0, 0.0).astype(F32)


def _shift_rows(x, prev_rows, k):
    rolled = pltpu.roll(x, k, axis=0)
    prev = pltpu.roll(prev_rows, k, axis=0)
    head = jnp.where(_iota2(prev.shape, 0) < k, prev, rolled[:8])
    return jnp.concatenate([head, rolled[8:]], axis=0)


def _silu(x):
    return x * jax.nn.sigmoid(x)


def _in_proj_kernel(x_ref, nw_ref, w_ref, hg_ref, rw_ref):
    x = x_ref[...]
    h = x * lax.rsqrt(jnp.mean(x * x, axis=-1, keepdims=True) + NORM_EPS) * nw_ref[...]
    hb = _bf(h)
    hg_ref[...] = jnp.dot(hb, w_ref[:, :HG_COLS], preferred_element_type=F32)
    rw_ref[...] = jnp.dot(hb, w_ref[:, HG_COLS:], preferred_element_type=F32)


def _in_proj(x2, norm_w, w_in_bf):
    m = x2.shape[0]
    return pl.pallas_call(
        _in_proj_kernel,
        grid=(m // ROW_TILE,),
        in_specs=[
            pl.BlockSpec((ROW_TILE, D_MODEL), lambda i: (i, 0)),
            pl.BlockSpec((1, D_MODEL), lambda i: (0, 0)),
            pl.BlockSpec((D_MODEL, IN_COLS), lambda i: (0, 0)),
        ],
        out_specs=[
            pl.BlockSpec((ROW_TILE, HG_COLS), lambda i: (i, 0)),
            pl.BlockSpec((ROW_TILE, RW_COLS), lambda i: (i, 0)),
        ],
        out_shape=[
            jax.ShapeDtypeStruct((m, HG_COLS), F32),
            jax.ShapeDtypeStruct((m, RW_COLS), F32),
        ],
        compiler_params=pltpu.CompilerParams(
            dimension_semantics=("arbitrary",), vmem_limit_bytes=VMEM_LIMIT_BYTES),
        name="in_proj",
    )(x2, norm_w, w_in_bf)


def _hgrn2_kernel(q_ref, f_ref, i_ref, g_ref, lbl_ref, nw_ref, o_ref, st_ref):
    @pl.when(pl.program_id(2) == 0)
    def _():
        st_ref[...] = jnp.zeros_like(st_ref)

    logits = lbl_ref[...]
    e = jnp.exp(logits - jnp.max(logits, axis=0, keepdims=True))
    lb = e[0:1] / jnp.sum(e, axis=0, keepdims=True)

    f = lb + (1.0 - lb) * jax.nn.sigmoid(f_ref[...])
    logf = jnp.log(f)
    kk = 1.0 - f
    q = _silu(q_ref[...]) * (HG_HEAD_DIM ** -0.5)
    v = i_ref[...]
    n = q.shape[0]
    b = _dot_exact_rhs(_chunk_tril(n), logf)

    tri = _iota2((CHUNK, CHUNK), 1) <= _iota2((CHUNK, CHUNK), 0)
    st = st_ref[...]
    outs = []
    for c in range(n // CHUNK):
        sl = slice(c * CHUNK, (c + 1) * CHUNK)
        bc = b[sl]
        b_last = bc[CHUNK - 1:CHUNK]
        qe = q[sl] * jnp.exp(bc)
        ke = kk[sl] * jnp.exp(-bc)
        kl = kk[sl] * jnp.exp(b_last - bc)
        vc = v[sl]
        a = jnp.where(tri, _dot_nt(qe, ke), 0.0)
        outs.append(_dot(a, vc) + _dot_nt(qe, st))
        st = st * jnp.exp(b_last) + _dot_tn(vc, kl)
    st_ref[...] = st

    o = jnp.concatenate(outs, axis=0)
    o = o * lax.rsqrt(jnp.mean(o * o, axis=-1, keepdims=True) + NORM_EPS)
    o_ref[...] = _bf(o * nw_ref[...] * _silu(g_ref[...]))


def _hgrn2(hg_p, lb_logits, norm_w, batch, seq):
    m = hg_p.shape[0]
    nt = seq // SCAN_TILE

    def col(section):
        return pl.BlockSpec((SCAN_TILE, LANES), lambda b, h, t: (b * nt + t, section * HG_HEADS + h))

    return pl.pallas_call(
        _hgrn2_kernel,
        grid=(batch, HG_HEADS, nt),
        in_specs=[
            col(0), col(1), col(2), col(3),
            pl.BlockSpec((2, LANES), lambda b, h, t: (0, h)),
            pl.BlockSpec((1, LANES), lambda b, h, t: (0, h)),
        ],
        out_specs=pl.BlockSpec((SCAN_TILE, LANES), lambda b, h, t: (b * nt + t, h)),
        out_shape=jax.ShapeDtypeStruct((m, HG_WIDTH), BF16),
        scratch_shapes=[pltpu.VMEM((HG_HEAD_DIM, HG_HEAD_DIM), F32)],
        compiler_params=pltpu.CompilerParams(
            dimension_semantics=("arbitrary", "arbitrary", "arbitrary"),
            vmem_limit_bytes=VMEM_LIMIT_BYTES),
        name="hgrn2",
    )(hg_p, hg_p, hg_p, hg_p, lb_logits, norm_w)


def _stack2(x):
    lane = _iota2(x.shape, 1)
    return jnp.concatenate([jnp.where(lane < RW_HEAD_DIM, x, 0.0),
                            jnp.where(lane >= RW_HEAD_DIM, x, 0.0)], axis=0)


def _unstack2(x):
    return x[:CHUNK] + x[CHUNK:]


def _rwkv7_kernel(r_ref, k_ref, v_ref, wag_ref, mu_r_ref, mu_k_ref, mu_v_ref, mu_wag_ref,
                  w0_ref, wwa_ref, a0_ref, g2_ref, kk_ref, ka_ref, rk_ref, lnw_ref, lnb_ref,
                  o_ref, z_ref, pr_ref, pk_ref, pv_ref, pwag_ref):
    @pl.when(pl.program_id(2) == 0)
    def _():
        z_ref[...] = jnp.zeros_like(z_ref)
        pr_ref[...] = jnp.zeros_like(pr_ref)
        pk_ref[...] = jnp.zeros_like(pk_ref)
        pv_ref[...] = jnp.zeros_like(pv_ref)
        pwag_ref[...] = jnp.zeros_like(pwag_ref)

    def mix(x_ref, prev_ref, mu_ref):
        x = x_ref[...]
        xs = _shift_rows(x, prev_ref[...], 1)
        prev_ref[...] = x[x.shape[0] - 8:]
        return x + (xs - x) * mu_ref[...]

    r = mix(r_ref, pr_ref, mu_r_ref)
    k = mix(k_ref, pk_ref, mu_k_ref)
    v = mix(v_ref, pv_ref, mu_v_ref)
    wag = mix(wag_ref, pwag_ref, mu_wag_ref)
    n = r.shape[0]

    wa = wag[:, :LANES]
    wa = jnp.where(_iota2(wa.shape, 1) < RW_DECAY_LORA, jnp.tanh(wa), wa)
    lora = _dot(wa, wwa_ref[...])
    zw = -(w0_ref[...] + lora[:, :LANES])
    softplus = jnp.maximum(zw, 0.0) + jnp.log(1.0 + jnp.exp(-jnp.abs(zw)))
    logw = -jnp.exp(-softplus - 0.5)
    a = jax.nn.sigmoid(a0_ref[...] + lora[:, LANES:])
    g = _dot(jax.nn.sigmoid(wag[:, LANES:]), g2_ref[...])

    lane = _iota2((LANES, LANES), 1)
    row = _iota2((LANES, LANES), 0)
    same_head = (row // RW_HEAD_DIM) == (lane // RW_HEAD_DIM)
    head_ones = jnp.where(same_head, 1.0, 0.0).astype(F32)

    kk = k * kk_ref[...]
    kk = kk / jnp.maximum(jnp.sqrt(_dot_exact_lhs(kk * kk, head_ones)), L2_EPS)
    k = k * (1.0 + (a - 1.0) * ka_ref[...])
    alpha = -kk
    beta = kk * a
    bonus = _dot_exact_lhs(r * k * rk_ref[...], head_ones) * v

    cum = _dot_exact_rhs(_chunk_tril(n), logw)

    strict = same_head & (lane % CHUNK < row % CHUNK)
    incl = same_head & (lane % CHUNK <= row % CHUNK)
    eye = lane == row

    z = z_ref[...]
    ys = []
    for c in range(n // CHUNK):
        sl = slice(c * CHUNK, (c + 1) * CHUNK)
        cc = cum[sl]
        c_last = cc[CHUNK - 1:CHUNK]
        e_pos = jnp.exp(cc)
        e_neg = jnp.exp(-cc)
        e_last = jnp.exp(c_last - cc)
        a_t = alpha[sl] * jnp.exp(cc - logw[sl])
        r_t = r[sl] * e_pos
        b_t = beta[sl] * e_neg
        k_t = k[sl] * e_neg
        b_h = beta[sl] * e_last
        k_h = k[sl] * e_last
        vc = v[sl]

        a_st, r_st, v_st = _stack2(a_t), _stack2(r_t), _stack2(vc)
        s4 = _dot_nt(jnp.concatenate([a_st, r_st], axis=0),
                     jnp.concatenate([_stack2(b_t), _stack2(k_t)], axis=0))
        l_ab = jnp.where(strict, s4[:LANES, :LANES], 0.0)
        l_ak = jnp.where(strict, s4[:LANES, LANES:], 0.0)
        a_rb = jnp.where(incl, s4[LANES:, :LANES], 0.0)
        a_rk = jnp.where(incl, s4[LANES:, LANES:], 0.0)

        tm = jnp.where(eye, 1.0, l_ab)
        lp = _dot(l_ab, l_ab)
        for _ in range(4):
            both = _dot(jnp.concatenate([tm, lp], axis=0), lp)
            tm = tm + both[:LANES]
            lp = both[LANES:]
        tm = tm + _dot(tm, lp)

        mp = _dot(tm, jnp.concatenate([a_st, _dot(l_ak, v_st)], axis=1))
        ry = _dot(a_rb, mp)
        rq = _unstack2(r_st + ry[:, :LANES])
        y0 = _unstack2(ry[:, LANES:] + _dot(a_rk, v_st))
        m1 = _unstack2(mp[:, :LANES])
        p0 = _unstack2(mp[:, LANES:])
        gu = _dot_tn(jnp.concatenate([b_h, k_h], axis=0),
                     jnp.concatenate([jnp.concatenate([m1, p0], axis=1),
                                      jnp.concatenate([jnp.zeros_like(vc), vc], axis=1)], axis=0))
        g_mat = jnp.where(same_head, gu[:, :LANES], 0.0) + jnp.where(eye, jnp.exp(c_last), 0.0)
        u_mat = jnp.where(same_head, gu[:, LANES:], 0.0)

        ys.append(_dot(rq, z) + y0)
        z = _dot(g_mat, z) + u_mat
    z_ref[...] = z

    y = jnp.concatenate(ys, axis=0)
    inv_n = 1.0 / RW_HEAD_DIM
    mean = _dot_exact_lhs(y, head_ones) * inv_n
    yc = y - mean
    var = _dot_exact_lhs(yc * yc, head_ones) * inv_n
    y = yc * lax.rsqrt(var + RW_GN_EPS) * lnw_ref[...] + lnb_ref[...]
    o_ref[...] = _bf((y + bonus) * g)


def _rwkv7(rw_p, mu, w0, wwa, a0, g2_bf, k_k, k_a, r_k, ln_w, ln_b, batch, seq):
    m = rw_p.shape[0]
    nt = seq // SCAN_TILE
    wag_w = 2 * LANES
    wag_blk = (3 * RW_WIDTH) // wag_w

    def col(section):
        return pl.BlockSpec((SCAN_TILE, LANES), lambda b, p, t: (b * nt + t, section * RW_PAIRS + p))

    def vec(section):
        return pl.BlockSpec((1, LANES), lambda b, p, t: (0, section * RW_PAIRS + p))

    def pvec():
        return pl.BlockSpec((1, LANES), lambda b, p, t: (0, p))

    return pl.pallas_call(
        _rwkv7_kernel,
        grid=(batch, RW_PAIRS, nt),
        in_specs=[
            col(0), col(1), col(2),
            pl.BlockSpec((SCAN_TILE, wag_w), lambda b, p, t: (b * nt + t, wag_blk)),
            vec(0), vec(1), vec(2),
            pl.BlockSpec((1, wag_w), lambda b, p, t: (0, wag_blk)),
            pvec(),
            pl.BlockSpec((None, LANES, 2 * LANES), lambda b, p, t: (p, 0, 0)),
            pvec(),
            pl.BlockSpec((RW_GATE_LORA, LANES), lambda b, p, t: (0, p)),
            pvec(), pvec(), pvec(), pvec(), pvec(),
        ],
        out_specs=pl.BlockSpec((SCAN_TILE, LANES), lambda b, p, t: (b * nt + t, p)),
        out_shape=jax.ShapeDtypeStruct((m, RW_WIDTH), BF16),
        scratch_shapes=[
            pltpu.VMEM((LANES, LANES), F32),
            pltpu.VMEM((8, LANES), F32), pltpu.VMEM((8, LANES), F32), pltpu.VMEM((8, LANES), F32),
            pltpu.VMEM((8, wag_w), F32),
        ],
        compiler_params=pltpu.CompilerParams(
            dimension_semantics=("arbitrary", "arbitrary", "arbitrary"),
            vmem_limit_bytes=VMEM_LIMIT_BYTES),
        name="rwkv7",
    )(rw_p, rw_p, rw_p, rw_p, mu, mu, mu, mu, w0, wwa, a0, g2_bf, k_k, k_a, r_k, ln_w, ln_b)


def _out_proj_kernel(x_ref, ohg_ref, orw_ref, w_ref, nw_ref, x1_ref, h2_ref):
    x1 = (x_ref[...]
          + jnp.dot(ohg_ref[...], w_ref[:HG_WIDTH], preferred_element_type=F32)
          + jnp.dot(orw_ref[...], w_ref[HG_WIDTH:], preferred_element_type=F32))
    x1_ref[...] = x1
    h2_ref[...] = _bf(x1 * lax.rsqrt(jnp.mean(x1 * x1, axis=-1, keepdims=True) + NORM_EPS) * nw_ref[...])


def _out_proj(x2, o_hg, o_rw, w_out_bf, norm_w):
    m = x2.shape[0]
    return pl.pallas_call(
        _out_proj_kernel,
        grid=(m // ROW_TILE,),
        in_specs=[
            pl.BlockSpec((ROW_TILE, D_MODEL), lambda i: (i, 0)),
            pl.BlockSpec((ROW_TILE, HG_WIDTH), lambda i: (i, 0)),
            pl.BlockSpec((ROW_TILE, RW_WIDTH), lambda i: (i, 0)),
            pl.BlockSpec((D_MODEL, D_MODEL), lambda i: (0, 0)),
            pl.BlockSpec((1, D_MODEL), lambda i: (0, 0)),
        ],
        out_specs=[
            pl.BlockSpec((ROW_TILE, D_MODEL), lambda i: (i, 0)),
            pl.BlockSpec((ROW_TILE, D_MODEL), lambda i: (i, 0)),
        ],
        out_shape=[
            jax.ShapeDtypeStruct((m, D_MODEL), F32),
            jax.ShapeDtypeStruct((m, D_MODEL), BF16),
        ],
        compiler_params=pltpu.CompilerParams(
            dimension_semantics=("arbitrary",), vmem_limit_bytes=VMEM_LIMIT_BYTES),
        name="out_proj",
    )(x2, o_hg, o_rw, w_out_bf, norm_w)


def _ffn_up_kernel(h_ref, w_ref, cw_ref, cb_ref, act_ref, prev_ref):
    @pl.when(pl.program_id(1) == 0)
    def _():
        prev_ref[...] = jnp.zeros_like(prev_ref)

    h = h_ref[...]
    n = h.shape[0]

    def conv(cols):
        u = jnp.dot(h, w_ref[:, cols], preferred_element_type=F32)
        prev = prev_ref[:, cols]
        prev_ref[:, cols] = u[n - 8:]
        cw = cw_ref[:, cols]
        return (cw[2:3] * u + cw[1:2] * _shift_rows(u, prev, 1) + cw[0:1] * _shift_rows(u, prev, 2)
                + cb_ref[:, cols])

    for j in range(D_FF // FFN_COL_TILE):
        gate = conv(slice(j * FFN_COL_TILE, (j + 1) * FFN_COL_TILE))
        val = conv(slice(D_FF + j * FFN_COL_TILE, D_FF + (j + 1) * FFN_COL_TILE))
        act_ref[:, j * FFN_COL_TILE:(j + 1) * FFN_COL_TILE] = _bf(_silu(gate) * val)


def _ffn_up(h2, w_up_bf, conv_w, conv_b, batch, seq):
    m = h2.shape[0]
    nt = seq // ROW_TILE
    return pl.pallas_call(
        _ffn_up_kernel,
        grid=(batch, nt),
        in_specs=[
            pl.BlockSpec((ROW_TILE, D_MODEL), lambda b, t: (b * nt + t, 0)),
            pl.BlockSpec((D_MODEL, 2 * D_FF), lambda b, t: (0, 0)),
            pl.BlockSpec((3, 2 * D_FF), lambda b, t: (0, 0)),
            pl.BlockSpec((1, 2 * D_FF), lambda b, t: (0, 0)),
        ],
        out_specs=pl.BlockSpec((ROW_TILE, D_FF), lambda b, t: (b * nt + t, 0)),
        out_shape=jax.ShapeDtypeStruct((m, D_FF), BF16),
        scratch_shapes=[pltpu.VMEM((8, 2 * D_FF), F32)],
        compiler_params=pltpu.CompilerParams(
            dimension_semantics=("arbitrary", "arbitrary"), vmem_limit_bytes=VMEM_LIMIT_BYTES),
        name="ffn_up",
    )(h2, w_up_bf, conv_w, conv_b)


def _ffn_down_kernel(act_ref, w_ref, x1_ref, nw_ref, o_ref):
    x2 = x1_ref[...] + jnp.dot(act_ref[...], w_ref[...], preferred_element_type=F32)
    o_ref[...] = x2 * lax.rsqrt(jnp.mean(x2 * x2, axis=-1, keepdims=True) + NORM_EPS) * nw_ref[...]


def _ffn_down(act, w_down_bf, x1, norm_w):
    m = act.shape[0]
    return pl.pallas_call(
        _ffn_down_kernel,
        grid=(m // ROW_TILE,),
        in_specs=[
            pl.BlockSpec((ROW_TILE, D_FF), lambda i: (i, 0)),
            pl.BlockSpec((D_FF, D_MODEL), lambda i: (0, 0)),
            pl.BlockSpec((ROW_TILE, D_MODEL), lambda i: (i, 0)),
            pl.BlockSpec((1, D_MODEL), lambda i: (0, 0)),
        ],
        out_specs=pl.BlockSpec((ROW_TILE, D_MODEL), lambda i: (i, 0)),
        out_shape=jax.ShapeDtypeStruct((m, D_MODEL), F32),
        compiler_params=pltpu.CompilerParams(
            dimension_semantics=("arbitrary",), vmem_limit_bytes=VMEM_LIMIT_BYTES),
        name="ffn_down",
    )(act, w_down_bf, x1, norm_w)


def kernel(x, norm1_w, w_in, hg_lb_logits, hg_norm_w, rw_shift_mu, rw_w0, rw_w2, rw_a0, rw_a2, rw_g2,
           rw_k_k, rw_k_a, rw_r_k, rw_ln_w, rw_ln_b, w_out, norm2_w, w_up, conv_w, conv_b, w_down,
           final_norm_w):
    batch, seq, _ = x.shape
    assert seq % SCAN_TILE == 0 and seq % ROW_TILE == 0 and norm1_w.shape[0] == 1
    assert CHUNK == RW_HEAD_DIM and 2 * RW_HEAD_DIM == LANES
    x2 = x.reshape(batch * seq, D_MODEL)

    w2 = rw_w2[0].reshape(RW_DECAY_LORA, RW_PAIRS, LANES).transpose(1, 0, 2)
    a2 = rw_a2[0].reshape(RW_AAA_LORA, RW_PAIRS, LANES).transpose(1, 0, 2)
    zeros = jnp.zeros_like(w2)
    wwa = _bf(jnp.concatenate([jnp.concatenate([w2, zeros], axis=2),
                               jnp.concatenate([zeros, a2], axis=2)], axis=1))

    hg_p, rw_p = _in_proj(x2, norm1_w, _bf(w_in[0]))
    o_hg = _hgrn2(hg_p, hg_lb_logits, hg_norm_w, batch, seq)
    o_rw = _rwkv7(rw_p, rw_shift_mu, rw_w0, wwa, rw_a0, _bf(rw_g2[0]), rw_k_k, rw_k_a, rw_r_k,
                  rw_ln_w, rw_ln_b, batch, seq)
    x1, h2 = _out_proj(x2, o_hg, o_rw, _bf(w_out[0]), norm2_w)
    act = _ffn_up(h2, _bf(w_up[0]), conv_w[0], conv_b, batch, seq)
    out = _ffn_down(act, _bf(w_down[0]), x1, final_norm_w.reshape(1, D_MODEL))
    return out.reshape(batch, seq, D_MODEL)
```

```python
import jax
import jax.numpy as jnp
from jax import lax
from jax.experimental import pallas as pl
from jax.experimental.pallas import tpu as pltpu

F32 = jnp.float32
BF16 = jnp.bfloat16

D_MODEL = 1024
HG_WIDTH = 512
HG_HEAD_DIM = 128
HG_HEADS = HG_WIDTH // HG_HEAD_DIM
RW_WIDTH = 512
RW_HEAD_DIM = 64
RW_DECAY_LORA = 64
RW_AAA_LORA = 64
RW_GATE_LORA = 128
RW_COLS = 3 * RW_WIDTH + RW_DECAY_LORA + RW_AAA_LORA + RW_GATE_LORA
HG_COLS = 4 * HG_WIDTH
IN_COLS = HG_COLS + RW_COLS
D_FF = 2816
NORM_EPS = 1e-6
RW_GN_EPS = 64e-5
L2_EPS = 1e-12

LANES = 128
CHUNK = 64
RW_PAIRS = RW_WIDTH // LANES
VMEM_LIMIT_BYTES = 56 * 1024 * 1024

ROW_TILE = 512
SCAN_TILE = 256
FFN_COL_TILE = 256


def _bf(x):
    return x.astype(BF16)


def _dot(a, b):
    return jnp.dot(_bf(a), _bf(b), preferred_element_type=F32)


def _dot_nt(a, b):
    return lax.dot_general(_bf(a), _bf(b), (((1,), (1,)), ((), ())), preferred_element_type=F32)


def _dot_tn(a, b):
    return jnp.dot(_bf(a.T), _bf(b), preferred_element_type=F32)


def _split3(x):
    h1 = _bf(x)
    r1 = x - h1.astype(F32)
    h2 = _bf(r1)
    r2 = r1 - h2.astype(F32)
    return h1, h2, _bf(r2)


def _dot_exact_rhs(m01, x):
    m = _bf(m01)
    h1, h2, h3 = _split3(x)
    return (jnp.dot(m, h1, preferred_element_type=F32)
            + jnp.dot(m, h2, preferred_element_type=F32)
            + jnp.dot(m, h3, preferred_element_type=F32))


def _dot_exact_lhs(x, m01):
    m = _bf(m01)
    h1, h2, h3 = _split3(x)
    return (jnp.dot(h1, m, preferred_element_type=F32)
            + jnp.dot(h2, m, preferred_element_type=F32)
            + jnp.dot(h3, m, preferred_element_type=F32))


def _iota2(shape, dim):
    return lax.broadcasted_iota(jnp.int32, shape, dim)


def _chunk_tril(n):
    r = _iota2((n, n), 0)
    c = _iota2((n, n), 1)
    return jnp.where((r // CHUNK == c // CHUNK) & (c <= r), 1.0, 0.0).astype(F32)


def _shift_rows(x, prev_rows, k):
    rolled = pltpu.roll(x, k, axis=0)
    prev = pltpu.roll(prev_rows, k, axis=0)
    head = jnp.where(_iota2(prev.shape, 0) < k, prev, rolled[:8])
    return jnp.concatenate([head, rolled[8:]], axis=0)


def _silu(x):
    return x * jax.nn.sigmoid(x)


def _in_proj_kernel(x_ref, nw_ref, w_ref, hg_ref, rw_ref):
    x = x_ref[...]
    h = x * lax.rsqrt(jnp.mean(x * x, axis=-1, keepdims=True) + NORM_EPS) * nw_ref[...]
    hb = _bf(h)
    hg_ref[...] = jnp.dot(hb, w_ref[:, :HG_COLS], preferred_element_type=F32)
    rw_ref[...] = jnp.dot(hb, w_ref[:, HG_COLS:], preferred_element_type=F32)


def _in_proj(x2, norm_w, w_in_bf):
    m = x2.shape[0]
    return pl.pallas_call(
        _in_proj_kernel,
        grid=(m // ROW_TILE,),
        in_specs=[
            pl.BlockSpec((ROW_TILE, D_MODEL), lambda i: (i, 0)),
            pl.BlockSpec((1, D_MODEL), lambda i: (0, 0)),
            pl.BlockSpec((D_MODEL, IN_COLS), lambda i: (0, 0)),
        ],
        out_specs=[
            pl.BlockSpec((ROW_TILE, HG_COLS), lambda i: (i, 0)),
            pl.BlockSpec((ROW_TILE, RW_COLS), lambda i: (i, 0)),
        ],
        out_shape=[
            jax.ShapeDtypeStruct((m, HG_COLS), F32),
            jax.ShapeDtypeStruct((m, RW_COLS), F32),
        ],
        compiler_params=pltpu.CompilerParams(
            dimension_semantics=("arbitrary",), vmem_limit_bytes=VMEM_LIMIT_BYTES),
        name="in_proj",
    )(x2, norm_w, w_in_bf)


def _hgrn2_kernel(q_ref, f_ref, i_ref, g_ref, lbl_ref, nw_ref, o_ref, st_ref):
    @pl.when(pl.program_id(2) == 0)
    def _():
        st_ref[...] = jnp.zeros_like(st_ref)

    logits = lbl_ref[...]
    e = jnp.exp(logits - jnp.max(logits, axis=0, keepdims=True))
    lb = e[0:1] / jnp.sum(e, axis=0, keepdims=True)

    f = lb + (1.0 - lb) * jax.nn.sigmoid(f_ref[...])
    logf = jnp.log(f)
    kk = 1.0 - f
    q = _silu(q_ref[...]) * (HG_HEAD_DIM ** -0.5)
    v = i_ref[...]
    n = q.shape[0]
    b = _dot_exact_rhs(_chunk_tril(n), logf)

    tri = _iota2((CHUNK, CHUNK), 1) <= _iota2((CHUNK, CHUNK), 0)
    st = st_ref[...]
    outs = []
    for c in range(n // CHUNK):
        sl = slice(c * CHUNK, (c + 1) * CHUNK)
        bc = b[sl]
        b_last = bc[CHUNK - 1:CHUNK]
        qe = q[sl] * jnp.exp(bc)
        ke = kk[sl] * jnp.exp(-bc)
        kl = kk[sl] * jnp.exp(b_last - bc)
        vc = v[sl]
        a = jnp.where(tri, _dot_nt(qe, ke), 0.0)
        outs.append(_dot(a, vc) + _dot_nt(qe, st))
        st = st * jnp.exp(b_last) + _dot_tn(vc, kl)
    st_ref[...] = st

    o = jnp.concatenate(outs, axis=0)
    o = o * lax.rsqrt(jnp.mean(o * o, axis=-1, keepdims=True) + NORM_EPS)
    o_ref[...] = _bf(o * nw_ref[...] * _silu(g_ref[...]))


def _hgrn2(hg_p, lb_logits, norm_w, batch, seq):
    m = hg_p.shape[0]
    nt = seq // SCAN_TILE

    def col(section):
        return pl.BlockSpec((SCAN_TILE, LANES), lambda b, h, t: (b * nt + t, section * HG_HEADS + h))

    return pl.pallas_call(
        _hgrn2_kernel,
        grid=(batch, HG_HEADS, nt),
        in_specs=[
            col(0), col(1), col(2), col(3),
            pl.BlockSpec((2, LANES), lambda b, h, t: (0, h)),
            pl.BlockSpec((1, LANES), lambda b, h, t: (0, h)),
        ],
        out_specs=pl.BlockSpec((SCAN_TILE, LANES), lambda b, h, t: (b * nt + t, h)),
        out_shape=jax.ShapeDtypeStruct((m, HG_WIDTH), BF16),
        scratch_shapes=[pltpu.VMEM((HG_HEAD_DIM, HG_HEAD_DIM), F32)],
        compiler_params=pltpu.CompilerParams(
            dimension_semantics=("arbitrary", "arbitrary", "arbitrary"),
            vmem_limit_bytes=VMEM_LIMIT_BYTES),
        name="hgrn2",
    )(hg_p, hg_p, hg_p, hg_p, lb_logits, norm_w)


def _stack2(x):
    lane = _iota2(x.shape, 1)
    return jnp.concatenate([jnp.where(lane < RW_HEAD_DIM, x, 0.0),
                            jnp.where(lane >= RW_HEAD_DIM, x, 0.0)], axis=0)


def _unstack2(x):
    return x[:CHUNK] + x[CHUNK:]


def _rwkv7_kernel(r_ref, k_ref, v_ref, wag_ref, mu_r_ref, mu_k_ref, mu_v_ref, mu_wag_ref,
                  w0_ref, wwa_ref, a0_ref, g2_ref, kk_ref, ka_ref, rk_ref, lnw_ref, lnb_ref,
                  o_ref, z_ref, pr_ref, pk_ref, pv_ref, pwag_ref):
    @pl.when(pl.program_id(2) == 0)
    def _():
        z_ref[...] = jnp.zeros_like(z_ref)
        pr_ref[...] = jnp.zeros_like(pr_ref)
        pk_ref[...] = jnp.zeros_like(pk_ref)
        pv_ref[...] = jnp.zeros_like(pv_ref)
        pwag_ref[...] = jnp.zeros_like(pwag_ref)

    def mix(x_ref, prev_ref, mu_ref):
        x = x_ref[...]
        xs = _shift_rows(x, prev_ref[...], 1)
        prev_ref[...] = x[x.shape[0] - 8:]
        return x + (xs - x) * mu_ref[...]

    r = mix(r_ref, pr_ref, mu_r_ref)
    k = mix(k_ref, pk_ref, mu_k_ref)
    v = mix(v_ref, pv_ref, mu_v_ref)
    wag = mix(wag_ref, pwag_ref, mu_wag_ref)
    n = r.shape[0]

    wa = wag[:, :LANES]
    wa = jnp.where(_iota2(wa.shape, 1) < RW_DECAY_LORA, jnp.tanh(wa), wa)
    lora = _dot(wa, wwa_ref[...])
    zw = -(w0_ref[...] + lora[:, :LANES])
    softplus = jnp.maximum(zw, 0.0) + jnp.log(1.0 + jnp.exp(-jnp.abs(zw)))
    logw = -jnp.exp(-softplus - 0.5)
    a = jax.nn.sigmoid(a0_ref[...] + lora[:, LANES:])
    g = _dot(jax.nn.sigmoid(wag[:, LANES:]), g2_ref[...])

    lane = _iota2((LANES, LANES), 1)
    row = _iota2((LANES, LANES), 0)
    same_head = (row // RW_HEAD_DIM) == (lane // RW_HEAD_DIM)
    head_ones = jnp.where(same_head, 1.0, 0.0).astype(F32)

    kk = k * kk_ref[...]
    kk = kk / jnp.maximum(jnp.sqrt(_dot_exact_lhs(kk * kk, head_ones)), L2_EPS)
    k = k * (1.0 + (a - 1.0) * ka_ref[...])
    alpha = -kk
    beta = kk * a
    bonus = _dot_exact_lhs(r * k * rk_ref[...], head_ones) * v

    cum = _dot_exact_rhs(_chunk_tril(n), logw)

    strict = same_head & (lane % CHUNK < row % CHUNK)
    incl = same_head & (lane % CHUNK <= row % CHUNK)
    eye = lane == row

    chunks = range(n // CHUNK)
    sls = [slice(c * CHUNK, (c + 1) * CHUNK) for c in chunks]
    cc = [cum[sl] for sl in sls]
    c_last = [x[CHUNK - 1:CHUNK] for x in cc]
    e_pos = [jnp.exp(x) for x in cc]
    e_neg = [jnp.exp(-x) for x in cc]
    e_last = [jnp.exp(cl - x) for cl, x in zip(c_last, cc)]
    a_st = [_stack2(alpha[sl] * jnp.exp(x - logw[sl])) for sl, x in zip(sls, cc)]
    r_st = [_stack2(r[sl] * e) for sl, e in zip(sls, e_pos)]
    b_st = [_stack2(beta[sl] * e) for sl, e in zip(sls, e_neg)]
    k_st = [_stack2(k[sl] * e) for sl, e in zip(sls, e_neg)]
    b_h = [beta[sl] * e for sl, e in zip(sls, e_last)]
    k_h = [k[sl] * e for sl, e in zip(sls, e_last)]
    vc = [v[sl] for sl in sls]
    v_st = [_stack2(x) for x in vc]

    s4 = [_dot_nt(jnp.concatenate([a, rr], axis=0), jnp.concatenate([b, kx], axis=0))
          for a, rr, b, kx in zip(a_st, r_st, b_st, k_st)]
    l_ab = [jnp.where(strict, s[:LANES, :LANES], 0.0) for s in s4]
    l_ak = [jnp.where(strict, s[:LANES, LANES:], 0.0) for s in s4]
    a_rb = [jnp.where(incl, s[LANES:, :LANES], 0.0) for s in s4]
    a_rk = [jnp.where(incl, s[LANES:, LANES:], 0.0) for s in s4]

    tm = [jnp.where(eye, 1.0, l) for l in l_ab]
    lp = [_dot(l, l) for l in l_ab]
    w1 = [_dot(l, x) for l, x in zip(l_ak, v_st)]
    y_kv = [_dot(l, x) for l, x in zip(a_rk, v_st)]
    for _ in range(4):
        both = [_dot(jnp.concatenate([t, l], axis=0), l) for t, l in zip(tm, lp)]
        tm = [t + x[:LANES] for t, x in zip(tm, both)]
        lp = [x[LANES:] for x in both]
    tm = [t + _dot(t, l) for t, l in zip(tm, lp)]

    mp = [_dot(t, jnp.concatenate([a, w], axis=1)) for t, a, w in zip(tm, a_st, w1)]
    ry = [_dot(l, x) for l, x in zip(a_rb, mp)]
    rq = [_unstack2(rr + x[:, :LANES]) for rr, x in zip(r_st, ry)]
    y0 = [_unstack2(x[:, LANES:] + yk) for x, yk in zip(ry, y_kv)]
    gu = [_dot_tn(jnp.concatenate([bh, kh], axis=0),
                  jnp.concatenate([_unstack2(x), jnp.concatenate([jnp.zeros_like(vv), vv], axis=1)], axis=0))
          for bh, kh, x, vv in zip(b_h, k_h, mp, vc)]
    g_mat = [jnp.where(same_head, x[:, :LANES], 0.0) + jnp.where(eye, jnp.exp(cl), 0.0)
             for x, cl in zip(gu, c_last)]
    u_mat = [jnp.where(same_head, x[:, LANES:], 0.0) for x in gu]

    z = z_ref[...]
    ys = []
    for c in chunks:
        ys.append(_dot(rq[c], z) + y0[c])
        z = _dot(g_mat[c], z) + u_mat[c]
    z_ref[...] = z

    y = jnp.concatenate(ys, axis=0)
    inv_n = 1.0 / RW_HEAD_DIM
    mean = _dot_exact_lhs(y, head_ones) * inv_n
    yc = y - mean
    var = _dot_exact_lhs(yc * yc, head_ones) * inv_n
    y = yc * lax.rsqrt(var + RW_GN_EPS) * lnw_ref[...] + lnb_ref[...]
    o_ref[...] = _bf((y + bonus) * g)


def _rwkv7(rw_p, mu, w0, wwa, a0, g2_bf, k_k, k_a, r_k, ln_w, ln_b, batch, seq):
    m = rw_p.shape[0]
    nt = seq // SCAN_TILE
    wag_w = 2 * LANES
    wag_blk = (3 * RW_WIDTH) // wag_w

    def col(section):
        return pl.BlockSpec((SCAN_TILE, LANES), lambda b, p, t: (b * nt + t, section * RW_PAIRS + p))

    def vec(section):
        return pl.BlockSpec((1, LANES), lambda b, p, t: (0, section * RW_PAIRS + p))

    def pvec():
        return pl.BlockSpec((1, LANES), lambda b, p, t: (0, p))

    return pl.pallas_call(
        _rwkv7_kernel,
        grid=(batch, RW_PAIRS, nt),
        in_specs=[
            col(0), col(1), col(2),
            pl.BlockSpec((SCAN_TILE, wag_w), lambda b, p, t: (b * nt + t, wag_blk)),
            vec(0), vec(1), vec(2),
            pl.BlockSpec((1, wag_w), lambda b, p, t: (0, wag_blk)),
            pvec(),
            pl.BlockSpec((None, LANES, 2 * LANES), lambda b, p, t: (p, 0, 0)),
            pvec(),
            pl.BlockSpec((RW_GATE_LORA, LANES), lambda b, p, t: (0, p)),
            pvec(), pvec(), pvec(), pvec(), pvec(),
        ],
        out_specs=pl.BlockSpec((SCAN_TILE, LANES), lambda b, p, t: (b * nt + t, p)),
        out_shape=jax.ShapeDtypeStruct((m, RW_WIDTH), BF16),
        scratch_shapes=[
            pltpu.VMEM((LANES, LANES), F32),
            pltpu.VMEM((8, LANES), F32), pltpu.VMEM((8, LANES), F32), pltpu.VMEM((8, LANES), F32),
            pltpu.VMEM((8, wag_w), F32),
        ],
        compiler_params=pltpu.CompilerParams(
            dimension_semantics=("arbitrary", "arbitrary", "arbitrary"),
            vmem_limit_bytes=VMEM_LIMIT_BYTES),
        name="rwkv7",
    )(rw_p, rw_p, rw_p, rw_p, mu, mu, mu, mu, w0, wwa, a0, g2_bf, k_k, k_a, r_k, ln_w, ln_b)


def _out_proj_kernel(x_ref, ohg_ref, orw_ref, w_ref, nw_ref, x1_ref, h2_ref):
    x1 = (x_ref[...]
          + jnp.dot(ohg_ref[...], w_ref[:HG_WIDTH], preferred_element_type=F32)
          + jnp.dot(orw_ref[...], w_ref[HG_WIDTH:], preferred_element_type=F32))
    x1_ref[...] = x1
    h2_ref[...] = _bf(x1 * lax.rsqrt(jnp.mean(x1 * x1, axis=-1, keepdims=True) + NORM_EPS) * nw_ref[...])


def _out_proj(x2, o_hg, o_rw, w_out_bf, norm_w):
    m = x2.shape[0]
    return pl.pallas_call(
        _out_proj_kernel,
        grid=(m // ROW_TILE,),
        in_specs=[
            pl.BlockSpec((ROW_TILE, D_MODEL), lambda i: (i, 0)),
            pl.BlockSpec((ROW_TILE, HG_WIDTH), lambda i: (i, 0)),
            pl.BlockSpec((ROW_TILE, RW_WIDTH), lambda i: (i, 0)),
            pl.BlockSpec((D_MODEL, D_MODEL), lambda i: (0, 0)),
            pl.BlockSpec((1, D_MODEL), lambda i: (0, 0)),
        ],
        out_specs=[
            pl.BlockSpec((ROW_TILE, D_MODEL), lambda i: (i, 0)),
            pl.BlockSpec((ROW_TILE, D_MODEL), lambda i: (i, 0)),
        ],
        out_shape=[
            jax.ShapeDtypeStruct((m, D_MODEL), F32),
            jax.ShapeDtypeStruct((m, D_MODEL), BF16),
        ],
        compiler_params=pltpu.CompilerParams(
            dimension_semantics=("arbitrary",), vmem_limit_bytes=VMEM_LIMIT_BYTES),
        name="out_proj",
    )(x2, o_hg, o_rw, w_out_bf, norm_w)


def _ffn_up_kernel(h_ref, w_ref, cw_ref, cb_ref, act_ref, prev_ref):
    @pl.when(pl.program_id(1) == 0)
    def _():
        prev_ref[...] = jnp.zeros_like(prev_ref)

    h = h_ref[...]
    n = h.shape[0]

    def conv(cols):
        u = jnp.dot(h, w_ref[:, cols], preferred_element_type=F32)
        prev = prev_ref[:, cols]
        prev_ref[:, cols] = u[n - 8:]
        cw = cw_ref[:, cols]
        return (cw[2:3] * u + cw[1:2] * _shift_rows(u, prev, 1) + cw[0:1] * _shift_rows(u, prev, 2)
                + cb_ref[:, cols])

    for j in range(D_FF // FFN_COL_TILE):
        gate = conv(slice(j * FFN_COL_TILE, (j + 1) * FFN_COL_TILE))
        val = conv(slice(D_FF + j * FFN_COL_TILE, D_FF + (j + 1) * FFN_COL_TILE))
        act_ref[:, j * FFN_COL_TILE:(j + 1) * FFN_COL_TILE] = _bf(_silu(gate) * val)


def _ffn_up(h2, w_up_bf, conv_w, conv_b, batch, seq):
    m = h2.shape[0]
    nt = seq // ROW_TILE
    return pl.pallas_call(
        _ffn_up_kernel,
        grid=(batch, nt),
        in_specs=[
            pl.BlockSpec((ROW_TILE, D_MODEL), lambda b, t: (b * nt + t, 0)),
            pl.BlockSpec((D_MODEL, 2 * D_FF), lambda b, t: (0, 0)),
            pl.BlockSpec((3, 2 * D_FF), lambda b, t: (0, 0)),
            pl.BlockSpec((1, 2 * D_FF), lambda b, t: (0, 0)),
        ],
        out_specs=pl.BlockSpec((ROW_TILE, D_FF), lambda b, t: (b * nt + t, 0)),
        out_shape=jax.ShapeDtypeStruct((m, D_FF), BF16),
        scratch_shapes=[pltpu.VMEM((8, 2 * D_FF), F32)],
        compiler_params=pltpu.CompilerParams(
            dimension_semantics=("arbitrary", "arbitrary"), vmem_limit_bytes=VMEM_LIMIT_BYTES),
        name="ffn_up",
    )(h2, w_up_bf, conv_w, conv_b)


def _ffn_down_kernel(act_ref, w_ref, x1_ref, nw_ref, o_ref):
    x2 = x1_ref[...] + jnp.dot(act_ref[...], w_ref[...], preferred_element_type=F32)
    o_ref[...] = x2 * lax.rsqrt(jnp.mean(x2 * x2, axis=-1, keepdims=True) + NORM_EPS) * nw_ref[...]


def _ffn_down(act, w_down_bf, x1, norm_w):
    m = act.shape[0]
    return pl.pallas_call(
        _ffn_down_kernel,
        grid=(m // ROW_TILE,),
        in_specs=[
            pl.BlockSpec((ROW_TILE, D_FF), lambda i: (i, 0)),
            pl.BlockSpec((D_FF, D_MODEL), lambda i: (0, 0)),
            pl.BlockSpec((ROW_TILE, D_MODEL), lambda i: (i, 0)),
            pl.BlockSpec((1, D_MODEL), lambda i: (0, 0)),
        ],
        out_specs=pl.BlockSpec((ROW_TILE, D_MODEL), lambda i: (i, 0)),
        out_shape=jax.ShapeDtypeStruct((m, D_MODEL), F32),
        compiler_params=pltpu.CompilerParams(
            dimension_semantics=("arbitrary",), vmem_limit_bytes=VMEM_LIMIT_BYTES),
        name="ffn_down",
    )(act, w_down_bf, x1, norm_w)


def kernel(x, norm1_w, w_in, hg_lb_logits, hg_norm_w, rw_shift_mu, rw_w0, rw_w2, rw_a0, rw_a2, rw_g2,
           rw_k_k, rw_k_a, rw_r_k, rw_ln_w, rw_ln_b, w_out, norm2_w, w_up, conv_w, conv_b, w_down,
           final_norm_w):
    batch, seq, _ = x.shape
    assert seq % SCAN_TILE == 0 and seq % ROW_TILE == 0 and norm1_w.shape[0] == 1
    assert CHUNK == RW_HEAD_DIM and 2 * RW_HEAD_DIM == LANES
    x2 = x.reshape(batch * seq, D_MODEL)

    w2 = rw_w2[0].reshape(RW_DECAY_LORA, RW_PAIRS, LANES).transpose(1, 0, 2)
    a2 = rw_a2[0].reshape(RW_AAA_LORA, RW_PAIRS, LANES).transpose(1, 0, 2)
    zeros = jnp.zeros_like(w2)
    wwa = _bf(jnp.concatenate([jnp.concatenate([w2, zeros], axis=2),
                               jnp.concatenate([zeros, a2], axis=2)], axis=1))

    hg_p, rw_p = _in_proj(x2, norm1_w, _bf(w_in[0]))
    o_hg = _hgrn2(hg_p, hg_lb_logits, hg_norm_w, batch, seq)
    o_rw = _rwkv7(rw_p, rw_shift_mu, rw_w0, wwa, rw_a0, _bf(rw_g2[0]), rw_k_k, rw_k_a, rw_r_k,
                  rw_ln_w, rw_ln_b, batch, seq)
    x1, h2 = _out_proj(x2, o_hg, o_rw, _bf(w_out[0]), norm2_w)
    act = _ffn_up(h2, _bf(w_up[0]), conv_w[0], conv_b, batch, seq)
    out = _ffn_down(act, _bf(w_down[0]), x1, final_norm_w.reshape(1, D_MODEL))
    return out.reshape(batch, seq, D_MODEL)
```

```python
import jax
import jax.numpy as jnp
from jax import lax
from jax.experimental import pallas as pl
from jax.experimental.pallas import tpu as pltpu

F32 = jnp.float32
BF16 = jnp.bfloat16

D_MODEL = 1024
HG_WIDTH = 512
HG_HEAD_DIM = 128
HG_HEADS = HG_WIDTH // HG_HEAD_DIM
RW_WIDTH = 512
RW_HEAD_DIM = 64
RW_DECAY_LORA = 64
RW_AAA_LORA = 64
RW_GATE_LORA = 128
RW_COLS = 3 * RW_WIDTH + RW_DECAY_LORA + RW_AAA_LORA + RW_GATE_LORA
HG_COLS = 4 * HG_WIDTH
IN_COLS = HG_COLS + RW_COLS
D_FF = 2816
NORM_EPS = 1e-6
RW_GN_EPS = 64e-5
L2_EPS = 1e-12

LANES = 128
CHUNK = 64
RW_PAIRS = RW_WIDTH // LANES
VMEM_LIMIT_BYTES = 56 * 1024 * 1024

ROW_TILE = 512
SCAN_TILE = 256
FFN_COL_TILE = 256


def _bf(x):
    return x.astype(BF16)


def _dot(a, b):
    return jnp.dot(_bf(a), _bf(b), preferred_element_type=F32)


def _dot_nt(a, b):
    return lax.dot_general(_bf(a), _bf(b), (((1,), (1,)), ((), ())), preferred_element_type=F32)


def _dot_tn(a, b):
    return jnp.dot(_bf(a.T), _bf(b), preferred_element_type=F32)


def _split3(x):
    h1 = _bf(x)
    r1 = x - h1.astype(F32)
    h2 = _bf(r1)
    r2 = r1 - h2.astype(F32)
    return h1, h2, _bf(r2)


def _dot_exact_rhs(m01, x):
    m = _bf(m01)
    h1, h2, h3 = _split3(x)
    return (jnp.dot(m, h1, preferred_element_type=F32)
            + jnp.dot(m, h2, preferred_element_type=F32)
            + jnp.dot(m, h3, preferred_element_type=F32))


def _dot_exact_lhs(x, m01):
    m = _bf(m01)
    h1, h2, h3 = _split3(x)
    return (jnp.dot(h1, m, preferred_element_type=F32)
            + jnp.dot(h2, m, preferred_element_type=F32)
            + jnp.dot(h3, m, preferred_element_type=F32))


def _iota2(shape, dim):
    return lax.broadcasted_iota(jnp.int32, shape, dim)


def _chunk_tril(n):
    r = _iota2((n, n), 0)
    c = _iota2((n, n), 1)
    return jnp.where((r // CHUNK == c // CHUNK) & (c <= r), 1.0, 0.0).astype(F32)


def _shift_rows(x, prev_rows, k):
    rolled = pltpu.roll(x, k, axis=0)
    prev = pltpu.roll(prev_rows, k, axis=0)
    head = jnp.where(_iota2(prev.shape, 0) < k, prev, rolled[:8])
    return jnp.concatenate([head, rolled[8:]], axis=0)


def _silu(x):
    return x * jax.nn.sigmoid(x)


def _in_proj_kernel(x_ref, nw_ref, w_ref, hg_ref, rw_ref):
    x = x_ref[...]
    h = x * lax.rsqrt(jnp.mean(x * x, axis=-1, keepdims=True) + NORM_EPS) * nw_ref[...]
    hb = _bf(h)
    hg_ref[...] = jnp.dot(hb, w_ref[:, :HG_COLS], preferred_element_type=F32)
    rw_ref[...] = jnp.dot(hb, w_ref[:, HG_COLS:], preferred_element_type=F32)


def _in_proj(x2, norm_w, w_in_bf):
    m = x2.shape[0]
    return pl.pallas_call(
        _in_proj_kernel,
        grid=(m // ROW_TILE,),
        in_specs=[
            pl.BlockSpec((ROW_TILE, D_MODEL), lambda i: (i, 0)),
            pl.BlockSpec((1, D_MODEL), lambda i: (0, 0)),
            pl.BlockSpec((D_MODEL, IN_COLS), lambda i: (0, 0)),
        ],
        out_specs=[
            pl.BlockSpec((ROW_TILE, HG_COLS), lambda i: (i, 0)),
            pl.BlockSpec((ROW_TILE, RW_COLS), lambda i: (i, 0)),
        ],
        out_shape=[
            jax.ShapeDtypeStruct((m, HG_COLS), F32),
            jax.ShapeDtypeStruct((m, RW_COLS), F32),
        ],
        compiler_params=pltpu.CompilerParams(
            dimension_semantics=("arbitrary",), vmem_limit_bytes=VMEM_LIMIT_BYTES),
        name="in_proj",
    )(x2, norm_w, w_in_bf)


def _hgrn2_kernel(x_ref, lbl_ref, nw_ref, o_ref, st_ref):
    @pl.when(pl.program_id(1) == 0)
    def _():
        st_ref[...] = jnp.zeros_like(st_ref)

    logits = lbl_ref[...]
    e = jnp.exp(logits - jnp.max(logits, axis=0, keepdims=True))
    lb = e[0:1] / jnp.sum(e, axis=0, keepdims=True)

    w = HG_WIDTH
    f = lb + (1.0 - lb) * jax.nn.sigmoid(x_ref[:, w:2 * w])
    logf = jnp.log(f)
    kk = 1.0 - f
    q = _silu(x_ref[:, :w]) * (HG_HEAD_DIM ** -0.5)
    v = x_ref[:, 2 * w:3 * w]
    n = q.shape[0]
    b = _dot_exact_rhs(_chunk_tril(n), logf)

    heads = range(HG_HEADS)
    chunks = range(n // CHUNK)
    probs = [(h, c) for h in heads for c in chunks]

    def blk(x, h, c):
        return x[c * CHUNK:(c + 1) * CHUNK, h * LANES:(h + 1) * LANES]

    mid = CHUNK // 2
    bc = [blk(b, h, c) for h, c in probs]
    b_mid = [x[mid - 1:mid] for x in bc]
    b_last = [x[CHUNK - 1:CHUNK] for x in bc]
    qs = [blk(q, h, c) * jnp.exp(x - m) for (h, c), x, m in zip(probs, bc, b_mid)]
    ks = [blk(kk, h, c) * jnp.exp(m - x) for (h, c), x, m in zip(probs, bc, b_mid)]
    qe = [x * jnp.exp(m) for x, m in zip(qs, b_mid)]
    kl = [x * jnp.exp(bl - m) for x, bl, m in zip(ks, b_last, b_mid)]
    vc = [blk(v, h, c) for h, c in probs]

    tri = _iota2((CHUNK, CHUNK), 1) <= _iota2((CHUNK, CHUNK), 0)
    a = [jnp.where(tri, _dot_nt(x, y), 0.0) for x, y in zip(qs, ks)]
    u = [_dot_tn(x, y) for x, y in zip(vc, kl)]
    o_intra = [_dot(x, y) for x, y in zip(a, vc)]

    st = [st_ref[h] for h in heads]
    outs = [[None] * len(chunks) for _ in heads]
    for c in chunks:
        for h in heads:
            j = h * len(chunks) + c
            outs[h][c] = o_intra[j] + _dot_nt(qe[j], st[h])
            st[h] = st[h] * jnp.exp(b_last[j]) + u[j]
    for h in heads:
        st_ref[h] = st[h]

    cols = []
    for h in heads:
        o = jnp.concatenate(outs[h], axis=0)
        cols.append(o * lax.rsqrt(jnp.mean(o * o, axis=-1, keepdims=True) + NORM_EPS))
    o = jnp.concatenate(cols, axis=1)
    o_ref[...] = _bf(o * nw_ref[...] * _silu(x_ref[:, 3 * w:]))


def _hgrn2(hg_p, lb_logits, norm_w, batch, seq):
    m = hg_p.shape[0]
    nt = seq // SCAN_TILE
    return pl.pallas_call(
        _hgrn2_kernel,
        grid=(batch, nt),
        in_specs=[
            pl.BlockSpec((SCAN_TILE, HG_COLS), lambda b, t: (b * nt + t, 0)),
            pl.BlockSpec((2, HG_WIDTH), lambda b, t: (0, 0)),
            pl.BlockSpec((1, HG_WIDTH), lambda b, t: (0, 0)),
        ],
        out_specs=pl.BlockSpec((SCAN_TILE, HG_WIDTH), lambda b, t: (b * nt + t, 0)),
        out_shape=jax.ShapeDtypeStruct((m, HG_WIDTH), BF16),
        scratch_shapes=[pltpu.VMEM((HG_HEADS, HG_HEAD_DIM, HG_HEAD_DIM), F32)],
        compiler_params=pltpu.CompilerParams(
            dimension_semantics=("arbitrary", "arbitrary"), vmem_limit_bytes=VMEM_LIMIT_BYTES),
        name="hgrn2",
    )(hg_p, lb_logits, norm_w)


def _stack2(x):
    lane = _iota2(x.shape, 1)
    return jnp.concatenate([jnp.where(lane < RW_HEAD_DIM, x, 0.0),
                            jnp.where(lane >= RW_HEAD_DIM, x, 0.0)], axis=0)


def _unstack2(x):
    return x[:CHUNK] + x[CHUNK:]


def _rwkv7_kernel(x_ref, mu_ref, w0_ref, wwa_ref, a0_ref, g2_ref, kk_ref, ka_ref, rk_ref, lnw_ref, lnb_ref,
                  o_ref, z_ref, prev_ref):
    @pl.when(pl.program_id(1) == 0)
    def _():
        z_ref[...] = jnp.zeros_like(z_ref)
        prev_ref[...] = jnp.zeros_like(prev_ref)

    x = x_ref[...]
    n = x.shape[0]
    xs = _shift_rows(x, prev_ref[...], 1)
    prev_ref[...] = x[n - 8:]
    x = x + (xs - x) * mu_ref[...]
    w = RW_WIDTH
    r, k, v = x[:, :w], x[:, w:2 * w], x[:, 2 * w:3 * w]
    wa = x[:, 3 * w:3 * w + LANES]
    gd = x[:, 3 * w + LANES:]

    wa = jnp.where(_iota2(wa.shape, 1) < RW_DECAY_LORA, jnp.tanh(wa), wa)
    lora = _dot(wa, wwa_ref[...])
    zw = -(w0_ref[...] + lora[:, :w])
    softplus = jnp.maximum(zw, 0.0) + jnp.log(1.0 + jnp.exp(-jnp.abs(zw)))
    logw = -jnp.exp(-softplus - 0.5)
    a = jax.nn.sigmoid(a0_ref[...] + lora[:, w:])
    g = _dot(jax.nn.sigmoid(gd), g2_ref[...])

    lane = _iota2((LANES, LANES), 1)
    row = _iota2((LANES, LANES), 0)
    same_head = (row // RW_HEAD_DIM) == (lane // RW_HEAD_DIM)
    head_ones = jnp.where(same_head, 1.0, 0.0).astype(F32)

    def head_sums(t):
        return jnp.concatenate(
            [_dot_exact_lhs(t[:, p * LANES:(p + 1) * LANES], head_ones) for p in range(RW_PAIRS)], axis=1)

    kk = k * kk_ref[...]
    kk = kk / jnp.maximum(jnp.sqrt(head_sums(kk * kk)), L2_EPS)
    k = k * (1.0 + (a - 1.0) * ka_ref[...])
    alpha = -kk
    beta = kk * a
    bonus = head_sums(r * k * rk_ref[...]) * v

    cum = _dot_exact_rhs(_chunk_tril(n), logw)
    e_pos = jnp.exp(cum)
    e_neg = jnp.exp(-cum)
    a_t = alpha * jnp.exp(cum - logw)
    r_t = r * e_pos
    b_t = beta * e_neg
    k_t = k * e_neg

    strict = same_head & (lane % CHUNK < row % CHUNK)
    incl = same_head & (lane % CHUNK <= row % CHUNK)
    eye = lane == row

    pairs = range(RW_PAIRS)
    chunks = range(n // CHUNK)
    probs = [(p, c) for p in pairs for c in chunks]

    def blk(t, p, c):
        return t[c * CHUNK:(c + 1) * CHUNK, p * LANES:(p + 1) * LANES]

    c_last = [blk(cum, p, c)[CHUNK - 1:CHUNK] for p, c in probs]
    e_last = [jnp.exp(cl - blk(cum, p, c)) for (p, c), cl in zip(probs, c_last)]
    a_st = [_stack2(blk(a_t, p, c)) for p, c in probs]
    r_st = [_stack2(blk(r_t, p, c)) for p, c in probs]
    b_st = [_stack2(blk(b_t, p, c)) for p, c in probs]
    k_st = [_stack2(blk(k_t, p, c)) for p, c in probs]
    b_h = [blk(beta, p, c) * e for (p, c), e in zip(probs, e_last)]
    k_h = [blk(k, p, c) * e for (p, c), e in zip(probs, e_last)]
    vc = [blk(v, p, c) for p, c in probs]
    v_st = [_stack2(t) for t in vc]

    s4 = [_dot_nt(jnp.concatenate([aa, rr], axis=0), jnp.concatenate([bb, kx], axis=0))
          for aa, rr, bb, kx in zip(a_st, r_st, b_st, k_st)]
    l_ab = [jnp.where(strict, s[:LANES, :LANES], 0.0) for s in s4]
    l_ak = [jnp.where(strict, s[:LANES, LANES:], 0.0) for s in s4]
    a_rb = [jnp.where(incl, s[LANES:, :LANES], 0.0) for s in s4]
    a_rk = [jnp.where(incl, s[LANES:, LANES:], 0.0) for s in s4]

    tm = [jnp.where(eye, 1.0, l) for l in l_ab]
    lp = [_dot(l, l) for l in l_ab]
    w1 = [_dot(l, t) for l, t in zip(l_ak, v_st)]
    y_kv = [_dot(l, t) for l, t in zip(a_rk, v_st)]
    for _ in range(4):
        both = [_dot(jnp.concatenate([t, l], axis=0), l) for t, l in zip(tm, lp)]
        tm = [t + bo[:LANES] for t, bo in zip(tm, both)]
        lp = [bo[LANES:] for bo in both]
    tm = [t + _dot(t, l) for t, l in zip(tm, lp)]

    mp = [_dot(t, jnp.concatenate([aa, ww], axis=1)) for t, aa, ww in zip(tm, a_st, w1)]
    ry = [_dot(l, t) for l, t in zip(a_rb, mp)]
    rq = [_unstack2(rr + t[:, :LANES]) for rr, t in zip(r_st, ry)]
    y0 = [_unstack2(t[:, LANES:] + yk) for t, yk in zip(ry, y_kv)]
    gu = [_dot_tn(jnp.concatenate([bh, kh], axis=0),
                  jnp.concatenate([_unstack2(t), jnp.concatenate([jnp.zeros_like(vv), vv], axis=1)], axis=0))
          for bh, kh, t, vv in zip(b_h, k_h, mp, vc)]
    g_mat = [jnp.where(same_head, t[:, :LANES], 0.0) + jnp.where(eye, jnp.exp(cl), 0.0)
             for t, cl in zip(gu, c_last)]
    u_mat = [jnp.where(same_head, t[:, LANES:], 0.0) for t in gu]

    z = [z_ref[p] for p in pairs]
    ys = [[None] * len(chunks) for _ in pairs]
    for c in chunks:
        for p in pairs:
            j = p * len(chunks) + c
            ys[p][c] = _dot(rq[j], z[p]) + y0[j]
            z[p] = _dot(g_mat[j], z[p]) + u_mat[j]
    for p in pairs:
        z_ref[p] = z[p]

    y = jnp.concatenate([jnp.concatenate(ys[p], axis=0) for p in pairs], axis=1)
    inv_n = 1.0 / RW_HEAD_DIM
    mean = head_sums(y) * inv_n
    yc = y - mean
    var = head_sums(yc * yc) * inv_n
    y = yc * lax.rsqrt(var + RW_GN_EPS) * lnw_ref[...] + lnb_ref[...]
    o_ref[...] = _bf((y + bonus) * g)


def _rwkv7(rw_p, mu, w0, wwa_bf, a0, g2_bf, k_k, k_a, r_k, ln_w, ln_b, batch, seq):
    m = rw_p.shape[0]
    nt = seq // SCAN_TILE

    def vec(width):
        return pl.BlockSpec((1, width), lambda b, t: (0, 0))

    return pl.pallas_call(
        _rwkv7_kernel,
        grid=(batch, nt),
        in_specs=[
            pl.BlockSpec((SCAN_TILE, RW_COLS), lambda b, t: (b * nt + t, 0)),
            vec(RW_COLS),
            vec(RW_WIDTH),
            pl.BlockSpec((LANES, 2 * RW_WIDTH), lambda b, t: (0, 0)),
            vec(RW_WIDTH),
            pl.BlockSpec((RW_GATE_LORA, RW_WIDTH), lambda b, t: (0, 0)),
            vec(RW_WIDTH), vec(RW_WIDTH), vec(RW_WIDTH), vec(RW_WIDTH), vec(RW_WIDTH),
        ],
        out_specs=pl.BlockSpec((SCAN_TILE, RW_WIDTH), lambda b, t: (b * nt + t, 0)),
        out_shape=jax.ShapeDtypeStruct((m, RW_WIDTH), BF16),
        scratch_shapes=[
            pltpu.VMEM((RW_PAIRS, LANES, LANES), F32),
            pltpu.VMEM((8, RW_COLS), F32),
        ],
        compiler_params=pltpu.CompilerParams(
            dimension_semantics=("arbitrary", "arbitrary"), vmem_limit_bytes=VMEM_LIMIT_BYTES),
        name="rwkv7",
    )(rw_p, mu, w0, wwa_bf, a0, g2_bf, k_k, k_a, r_k, ln_w, ln_b)


def _out_proj_kernel(x_ref, ohg_ref, orw_ref, w_ref, nw_ref, x1_ref, h2_ref):
    x1 = (x_ref[...]
          + jnp.dot(ohg_ref[...], w_ref[:HG_WIDTH], preferred_element_type=F32)
          + jnp.dot(orw_ref[...], w_ref[HG_WIDTH:], preferred_element_type=F32))
    x1_ref[...] = x1
    h2_ref[...] = _bf(x1 * lax.rsqrt(jnp.mean(x1 * x1, axis=-1, keepdims=True) + NORM_EPS) * nw_ref[...])


def _out_proj(x2, o_hg, o_rw, w_out_bf, norm_w):
    m = x2.shape[0]
    return pl.pallas_call(
        _out_proj_kernel,
        grid=(m // ROW_TILE,),
        in_specs=[
            pl.BlockSpec((ROW_TILE, D_MODEL), lambda i: (i, 0)),
            pl.BlockSpec((ROW_TILE, HG_WIDTH), lambda i: (i, 0)),
            pl.BlockSpec((ROW_TILE, RW_WIDTH), lambda i: (i, 0)),
            pl.BlockSpec((D_MODEL, D_MODEL), lambda i: (0, 0)),
            pl.BlockSpec((1, D_MODEL), lambda i: (0, 0)),
        ],
        out_specs=[
            pl.BlockSpec((ROW_TILE, D_MODEL), lambda i: (i, 0)),
            pl.BlockSpec((ROW_TILE, D_MODEL), lambda i: (i, 0)),
        ],
        out_shape=[
            jax.ShapeDtypeStruct((m, D_MODEL), F32),
            jax.ShapeDtypeStruct((m, D_MODEL), BF16),
        ],
        compiler_params=pltpu.CompilerParams(
            dimension_semantics=("arbitrary",), vmem_limit_bytes=VMEM_LIMIT_BYTES),
        name="out_proj",
    )(x2, o_hg, o_rw, w_out_bf, norm_w)


def _ffn_up_kernel(h_ref, w_ref, cw_ref, cb_ref, act_ref, prev_ref, u_ref):
    @pl.when(pl.program_id(1) == 0)
    def _():
        prev_ref[...] = jnp.zeros_like(prev_ref)

    h = h_ref[...]
    n = h.shape[0]

    for j in range(D_FF // FFN_COL_TILE):
        cols = slice(2 * j * FFN_COL_TILE, 2 * (j + 1) * FFN_COL_TILE)
        u = u_ref.at[j % 2]
        u[0:8, :] = prev_ref[:, cols]
        u[8:, :] = jnp.dot(h, w_ref[:, cols], preferred_element_type=F32)
        prev_ref[:, cols] = u[n:n + 8, :]
        cw = cw_ref[:, cols]
        c = cw[2:3] * u[8:n + 8, :] + cw[1:2] * u[7:n + 7, :] + cw[0:1] * u[6:n + 6, :] + cb_ref[:, cols]
        act_ref[:, j * FFN_COL_TILE:(j + 1) * FFN_COL_TILE] = _bf(
            _silu(c[:, :FFN_COL_TILE]) * c[:, FFN_COL_TILE:])


def _ffn_up(h2, w_up_bf, conv_w, conv_b, batch, seq):
    m = h2.shape[0]
    nt = seq // ROW_TILE
    return pl.pallas_call(
        _ffn_up_kernel,
        grid=(batch, nt),
        in_specs=[
            pl.BlockSpec((ROW_TILE, D_MODEL), lambda b, t: (b * nt + t, 0)),
            pl.BlockSpec((D_MODEL, 2 * D_FF), lambda b, t: (0, 0)),
            pl.BlockSpec((3, 2 * D_FF), lambda b, t: (0, 0)),
            pl.BlockSpec((1, 2 * D_FF), lambda b, t: (0, 0)),
        ],
        out_specs=pl.BlockSpec((ROW_TILE, D_FF), lambda b, t: (b * nt + t, 0)),
        out_shape=jax.ShapeDtypeStruct((m, D_FF), BF16),
        scratch_shapes=[pltpu.VMEM((8, 2 * D_FF), F32),
                        pltpu.VMEM((2, 8 + ROW_TILE, 2 * FFN_COL_TILE), F32)],
        compiler_params=pltpu.CompilerParams(
            dimension_semantics=("arbitrary", "arbitrary"), vmem_limit_bytes=VMEM_LIMIT_BYTES),
        name="ffn_up",
    )(h2, w_up_bf, conv_w, conv_b)


def _ffn_down_kernel(act_ref, w_ref, x1_ref, nw_ref, o_ref):
    x2 = x1_ref[...] + jnp.dot(act_ref[...], w_ref[...], preferred_element_type=F32)
    o_ref[...] = x2 * lax.rsqrt(jnp.mean(x2 * x2, axis=-1, keepdims=True) + NORM_EPS) * nw_ref[...]


def _ffn_down(act, w_down_bf, x1, norm_w):
    m = act.shape[0]
    return pl.pallas_call(
        _ffn_down_kernel,
        grid=(m // ROW_TILE,),
        in_specs=[
            pl.BlockSpec((ROW_TILE, D_FF), lambda i: (i, 0)),
            pl.BlockSpec((D_FF, D_MODEL), lambda i: (0, 0)),
            pl.BlockSpec((ROW_TILE, D_MODEL), lambda i: (i, 0)),
            pl.BlockSpec((1, D_MODEL), lambda i: (0, 0)),
        ],
        out_specs=pl.BlockSpec((ROW_TILE, D_MODEL), lambda i: (i, 0)),
        out_shape=jax.ShapeDtypeStruct((m, D_MODEL), F32),
        compiler_params=pltpu.CompilerParams(
            dimension_semantics=("arbitrary",), vmem_limit_bytes=VMEM_LIMIT_BYTES),
        name="ffn_down",
    )(act, w_down_bf, x1, norm_w)


def kernel(x, norm1_w, w_in, hg_lb_logits, hg_norm_w, rw_shift_mu, rw_w0, rw_w2, rw_a0, rw_a2, rw_g2,
           rw_k_k, rw_k_a, rw_r_k, rw_ln_w, rw_ln_b, w_out, norm2_w, w_up, conv_w, conv_b, w_down,
           final_norm_w):
    batch, seq, _ = x.shape
    assert seq % SCAN_TILE == 0 and seq % ROW_TILE == 0 and norm1_w.shape[0] == 1
    assert CHUNK == RW_HEAD_DIM and 2 * RW_HEAD_DIM == LANES
    x2 = x.reshape(batch * seq, D_MODEL)

    zeros = jnp.zeros_like(rw_w2[0])
    wwa = _bf(jnp.concatenate([jnp.concatenate([rw_w2[0], zeros], axis=1),
                               jnp.concatenate([zeros, rw_a2[0]], axis=1)], axis=0))

    hg_p, rw_p = _in_proj(x2, norm1_w, _bf(w_in[0]))
    o_hg = _hgrn2(hg_p, hg_lb_logits, hg_norm_w, batch, seq)
    o_rw = _rwkv7(rw_p, rw_shift_mu, rw_w0, wwa, rw_a0, _bf(rw_g2[0]), rw_k_k, rw_k_a, rw_r_k,
                  rw_ln_w, rw_ln_b, batch, seq)
    x1, h2 = _out_proj(x2, o_hg, o_rw, _bf(w_out[0]), norm2_w)
    def tile_interleave(t):
        rows = t.shape[0]
        t = t.reshape(rows, 2, D_FF // FFN_COL_TILE, FFN_COL_TILE)
        return t.transpose(0, 2, 1, 3).reshape(rows, 2 * D_FF)

    act = _ffn_up(h2, _bf(tile_interleave(w_up[0])), tile_interleave(conv_w[0]), tile_interleave(conv_b),
                  batch, seq)
    out = _ffn_down(act, _bf(w_down[0]), x1, final_norm_w.reshape(1, D_MODEL))
    return out.reshape(batch, seq, D_MODEL)
```

```python
import jax
import jax.numpy as jnp
from jax import lax
from jax.experimental import pallas as pl
from jax.experimental.pallas import tpu as pltpu

F32 = jnp.float32
BF16 = jnp.bfloat16

D_MODEL = 1024
HG_WIDTH = 512
HG_HEAD_DIM = 128
HG_HEADS = HG_WIDTH // HG_HEAD_DIM
RW_WIDTH = 512
RW_HEAD_DIM = 64
RW_DECAY_LORA = 64
RW_AAA_LORA = 64
RW_GATE_LORA = 128
RW_COLS = 3 * RW_WIDTH + RW_DECAY_LORA + RW_AAA_LORA + RW_GATE_LORA
HG_COLS = 4 * HG_WIDTH
IN_COLS = HG_COLS + RW_COLS
D_FF = 2816
NORM_EPS = 1e-6
RW_GN_EPS = 64e-5
L2_EPS = 1e-12

LANES = 128
CHUNK = 64
RW_PAIRS = RW_WIDTH // LANES
VMEM_LIMIT_BYTES = 56 * 1024 * 1024

ROW_TILE = 512
SCAN_TILE = 256
FFN_COL_TILE = 256


def _bf(x):
    return x.astype(BF16)


def _dot(a, b):
    return jnp.dot(_bf(a), _bf(b), preferred_element_type=F32)


def _dot_nt(a, b):
    return lax.dot_general(_bf(a), _bf(b), (((1,), (1,)), ((), ())), preferred_element_type=F32)


def _dot_tn(a, b):
    return jnp.dot(_bf(a.T), _bf(b), preferred_element_type=F32)


def _split3(x):
    h1 = _bf(x)
    r1 = x - h1.astype(F32)
    h2 = _bf(r1)
    r2 = r1 - h2.astype(F32)
    return h1, h2, _bf(r2)


def _dot_exact_rhs(m01, x):
    m = _bf(m01)
    h1, h2, h3 = _split3(x)
    return (jnp.dot(m, h1, preferred_element_type=F32)
            + jnp.dot(m, h2, preferred_element_type=F32)
            + jnp.dot(m, h3, preferred_element_type=F32))


def _dot_exact_lhs(x, m01):
    m = _bf(m01)
    h1, h2, h3 = _split3(x)
    return (jnp.dot(h1, m, preferred_element_type=F32)
            + jnp.dot(h2, m, preferred_element_type=F32)
            + jnp.dot(h3, m, preferred_element_type=F32))


def _iota2(shape, dim):
    return lax.broadcasted_iota(jnp.int32, shape, dim)


def _chunk_tril(n):
    r = _iota2((n, n), 0)
    c = _iota2((n, n), 1)
    return jnp.where((r // CHUNK == c // CHUNK) & (c <= r), 1.0, 0.0).astype(F32)


def _shift_rows(x, prev_rows, k):
    rolled = pltpu.roll(x, k, axis=0)
    prev = pltpu.roll(prev_rows, k, axis=0)
    head = jnp.where(_iota2(prev.shape, 0) < k, prev, rolled[:8])
    return jnp.concatenate([head, rolled[8:]], axis=0)


def _silu(x):
    return x * jax.nn.sigmoid(x)


def _in_proj_kernel(x_ref, nw_ref, w_ref, hg_ref, rw_ref):
    x = x_ref[...]
    h = x * lax.rsqrt(jnp.mean(x * x, axis=-1, keepdims=True) + NORM_EPS) * nw_ref[...]
    hb = _bf(h)
    hg_ref[...] = jnp.dot(hb, w_ref[:, :HG_COLS], preferred_element_type=F32)
    rw_ref[...] = jnp.dot(hb, w_ref[:, HG_COLS:], preferred_element_type=F32)


def _in_proj(x2, norm_w, w_in_bf):
    m = x2.shape[0]
    return pl.pallas_call(
        _in_proj_kernel,
        grid=(m // ROW_TILE,),
        in_specs=[
            pl.BlockSpec((ROW_TILE, D_MODEL), lambda i: (i, 0)),
            pl.BlockSpec((1, D_MODEL), lambda i: (0, 0)),
            pl.BlockSpec((D_MODEL, IN_COLS), lambda i: (0, 0)),
        ],
        out_specs=[
            pl.BlockSpec((ROW_TILE, HG_COLS), lambda i: (i, 0)),
            pl.BlockSpec((ROW_TILE, RW_COLS), lambda i: (i, 0)),
        ],
        out_shape=[
            jax.ShapeDtypeStruct((m, HG_COLS), F32),
            jax.ShapeDtypeStruct((m, RW_COLS), F32),
        ],
        compiler_params=pltpu.CompilerParams(
            dimension_semantics=("arbitrary",), vmem_limit_bytes=VMEM_LIMIT_BYTES),
        name="in_proj",
    )(x2, norm_w, w_in_bf)


def _hgrn2_kernel(x_ref, lbl_ref, nw_ref, o_ref, st_ref):
    @pl.when(pl.program_id(1) == 0)
    def _():
        st_ref[...] = jnp.zeros_like(st_ref)

    logits = lbl_ref[...]
    e = jnp.exp(logits - jnp.max(logits, axis=0, keepdims=True))
    lb = e[0:1] / jnp.sum(e, axis=0, keepdims=True)

    w = HG_WIDTH
    f = lb + (1.0 - lb) * jax.nn.sigmoid(x_ref[:, w:2 * w])
    kk = 1.0 - f
    q = _silu(x_ref[:, :w]) * (HG_HEAD_DIM ** -0.5)
    v = x_ref[:, 2 * w:3 * w]
    n = q.shape[0]
    b = _dot_exact_rhs(_chunk_tril(n), jnp.log2(f))

    heads = range(HG_HEADS)
    chunks = range(n // CHUNK)
    probs = [(h, c) for h in heads for c in chunks]

    def blk(x, h, c):
        return x[c * CHUNK:(c + 1) * CHUNK, h * LANES:(h + 1) * LANES]

    def bcast_rows(x, period, offset):
        return jnp.concatenate(
            [jnp.broadcast_to(x[g * period + offset:g * period + offset + 1], (period, x.shape[1]))
             for g in range(x.shape[0] // period)], axis=0)

    t_idx = _iota2(b.shape, 0)
    srow = _iota2((CHUNK, CHUNK), 0)
    scol = _iota2((CHUNK, CHUNK), 1)
    q_lv, k_lv, sel = [q * f], [kk], []
    size = 2
    while size <= CHUNK:
        half = size // 2
        if size == 4:
            pos = t_idx % size
            m = jnp.where(pos == 0, pltpu.roll(b, n - 1, axis=0),
                          jnp.where(pos == 1, b,
                                    jnp.where(pos == 2, pltpu.roll(b, 1, axis=0), pltpu.roll(b, 2, axis=0))))
        elif size > 4:
            m = bcast_rows(b, size, half - 1)
        if size > 2:
            ez = jnp.exp2(-jnp.abs(b - m))
            q_lv.append(q * ez)
            k_lv.append(kk * ez)
        sel.append((srow // size == scol // size) & (srow % size >= half) & (scol % size < half))
        size *= 2
    qe_all = q * jnp.exp2(b)
    kl_all = kk * jnp.exp2(bcast_rows(b, CHUNK, CHUNK - 1) - b)

    b_last = [blk(b, h, c)[CHUNK - 1:CHUNK] for h, c in probs]
    qe = [blk(qe_all, h, c) for h, c in probs]
    vc = [blk(v, h, c) for h, c in probs]

    s_lv = [[_dot_nt(blk(ql, h, c), blk(kl, h, c)) for h, c in probs] for ql, kl in zip(q_lv, k_lv)]
    a = []
    for j in range(len(probs)):
        acc = jnp.zeros((CHUNK, CHUNK), F32)
        for lv in range(len(sel)):
            acc = jnp.where(sel[lv], s_lv[lv][j], acc)
        a.append(acc)
    u = [_dot_tn(x, blk(kl_all, h, c)) for x, (h, c) in zip(vc, probs)]
    qk = q * kk
    o_intra = [_dot(x, y) + jnp.sum(blk(qk, h, c), axis=-1, keepdims=True) * y
               for x, y, (h, c) in zip(a, vc, probs)]

    st = [st_ref[h] for h in heads]
    outs = [[None] * len(chunks) for _ in heads]
    for c in chunks:
        for h in heads:
            j = h * len(chunks) + c
            outs[h][c] = o_intra[j] + _dot_nt(qe[j], st[h])
            st[h] = st[h] * jnp.exp2(b_last[j]) + u[j]
    for h in heads:
        st_ref[h] = st[h]

    cols = []
    for h in heads:
        o = jnp.concatenate(outs[h], axis=0)
        cols.append(o * lax.rsqrt(jnp.mean(o * o, axis=-1, keepdims=True) + NORM_EPS))
    o = jnp.concatenate(cols, axis=1)
    o_ref[...] = _bf(o * nw_ref[...] * _silu(x_ref[:, 3 * w:]))


def _hgrn2(hg_p, lb_logits, norm_w, batch, seq):
    m = hg_p.shape[0]
    nt = seq // SCAN_TILE
    return pl.pallas_call(
        _hgrn2_kernel,
        grid=(batch, nt),
        in_specs=[
            pl.BlockSpec((SCAN_TILE, HG_COLS), lambda b, t: (b * nt + t, 0)),
            pl.BlockSpec((2, HG_WIDTH), lambda b, t: (0, 0)),
            pl.BlockSpec((1, HG_WIDTH), lambda b, t: (0, 0)),
        ],
        out_specs=pl.BlockSpec((SCAN_TILE, HG_WIDTH), lambda b, t: (b * nt + t, 0)),
        out_shape=jax.ShapeDtypeStruct((m, HG_WIDTH), BF16),
        scratch_shapes=[pltpu.VMEM((HG_HEADS, HG_HEAD_DIM, HG_HEAD_DIM), F32)],
        compiler_params=pltpu.CompilerParams(
            dimension_semantics=("arbitrary", "arbitrary"), vmem_limit_bytes=VMEM_LIMIT_BYTES),
        name="hgrn2",
    )(hg_p, lb_logits, norm_w)


def _stack2(x):
    lane = _iota2(x.shape, 1)
    return jnp.concatenate([jnp.where(lane < RW_HEAD_DIM, x, 0.0),
                            jnp.where(lane >= RW_HEAD_DIM, x, 0.0)], axis=0)


def _rwkv7_kernel(x_ref, mu_ref, w0_ref, wwa_ref, a0_ref, g2_ref, kk_ref, ka_ref, rk_ref, lnw_ref, lnb_ref,
                  o_ref, z_ref, prev_ref):
    @pl.when(pl.program_id(1) == 0)
    def _():
        z_ref[...] = jnp.zeros_like(z_ref)
        prev_ref[...] = jnp.zeros_like(prev_ref)

    x = x_ref[...]
    n = x.shape[0]
    xs = _shift_rows(x, prev_ref[...], 1)
    prev_ref[...] = x[n - 8:]
    x = x + (xs - x) * mu_ref[...]
    w = RW_WIDTH
    r, k, v = x[:, :w], x[:, w:2 * w], x[:, 2 * w:3 * w]
    wa = x[:, 3 * w:3 * w + LANES]
    gd = x[:, 3 * w + LANES:]

    wa = jnp.where(_iota2(wa.shape, 1) < RW_DECAY_LORA, jnp.tanh(wa), wa)
    lora = _dot(wa, wwa_ref[...])
    zw = -(w0_ref[...] + lora[:, :w])
    softplus = jnp.maximum(zw, 0.0) + jnp.log(1.0 + jnp.exp(-jnp.abs(zw)))
    logw = -jnp.exp(-softplus - 0.5)
    a = jax.nn.sigmoid(a0_ref[...] + lora[:, w:])
    g = _dot(jax.nn.sigmoid(gd), g2_ref[...])

    lane = _iota2((LANES, LANES), 1)
    row = _iota2((LANES, LANES), 0)
    same_head = (row // RW_HEAD_DIM) == (lane // RW_HEAD_DIM)
    head_ones = jnp.where(same_head, 1.0, 0.0).astype(F32)

    def head_sums(t):
        return jnp.concatenate(
            [_dot_exact_lhs(t[:, p * LANES:(p + 1) * LANES], head_ones) for p in range(RW_PAIRS)], axis=1)

    kk = k * kk_ref[...]
    kk = kk / jnp.maximum(jnp.sqrt(head_sums(kk * kk)), L2_EPS)
    k = k * (1.0 + (a - 1.0) * ka_ref[...])
    alpha = -kk
    beta = kk * a
    bonus = head_sums(r * k * rk_ref[...]) * v

    cum = _dot_exact_rhs(_chunk_tril(n), logw)
    e_pos = jnp.exp(cum)
    e_neg = jnp.exp(-cum)
    a_t = alpha * jnp.exp(cum - logw)
    r_t = r * e_pos
    b_t = beta * e_neg
    k_t = k * e_neg

    wrow = _iota2((CHUNK, LANES), 0)
    wcol = _iota2((CHUNK, LANES), 1) % CHUNK
    strict = wcol < wrow
    incl = wcol <= wrow
    eye_w = jnp.where(wcol == wrow, 1.0, 0.0).astype(F32)
    eye = lane == row

    pairs = range(RW_PAIRS)
    chunks = range(n // CHUNK)
    probs = [(p, c) for p in pairs for c in chunks]

    def blk(t, p, c):
        return t[c * CHUNK:(c + 1) * CHUNK, p * LANES:(p + 1) * LANES]

    c_last = [blk(cum, p, c)[CHUNK - 1:CHUNK] for p, c in probs]
    e_last = [jnp.exp(cl - blk(cum, p, c)) for (p, c), cl in zip(probs, c_last)]
    a_c = [blk(a_t, p, c) for p, c in probs]
    r_c = [blk(r_t, p, c) for p, c in probs]
    b_h = [blk(beta, p, c) * e for (p, c), e in zip(probs, e_last)]
    k_h = [blk(k, p, c) * e for (p, c), e in zip(probs, e_last)]
    vc = [blk(v, p, c) for p, c in probs]
    v_st = [_stack2(t) for t in vc]

    s4 = [_dot_nt(jnp.concatenate([aa, rr], axis=0),
                  jnp.concatenate([_stack2(blk(b_t, p, c)), _stack2(blk(k_t, p, c))], axis=0))
          for aa, rr, (p, c) in zip(a_c, r_c, probs)]
    l_ab = [jnp.where(strict, s[:CHUNK, :LANES], 0.0) for s in s4]
    l_ak = [jnp.where(strict, s[:CHUNK, LANES:], 0.0) for s in s4]
    a_rb = [jnp.where(incl, s[CHUNK:, :LANES], 0.0) for s in s4]
    a_rk = [jnp.where(incl, s[CHUNK:, LANES:], 0.0) for s in s4]

    tm = [l + eye_w for l in l_ab]
    lp = [_dot(l, _stack2(l)) for l in l_ab]
    w1 = [_dot(l, t) for l, t in zip(l_ak, v_st)]
    for _ in range(4):
        both = [_dot(jnp.concatenate([t, l], axis=0), _stack2(l)) for t, l in zip(tm, lp)]
        tm = [t + bo[:CHUNK] for t, bo in zip(tm, both)]
        lp = [bo[CHUNK:] for bo in both]
    tm = [t + _dot(t, _stack2(l)) for t, l in zip(tm, lp)]

    mp = [_dot(t, jnp.concatenate([_stack2(aa), _stack2(ww)], axis=1))
          for t, aa, ww in zip(tm, a_c, w1)]
    ry = [_dot(jnp.concatenate([lb, lk], axis=1),
               jnp.concatenate([jnp.concatenate([_stack2(t[:, :LANES]), _stack2(t[:, LANES:])], axis=1),
                                jnp.concatenate([jnp.zeros_like(vs), vs], axis=1)], axis=0))
          for lb, lk, t, vs in zip(a_rb, a_rk, mp, v_st)]
    rq = [rr + t[:, :LANES] for rr, t in zip(r_c, ry)]
    y0 = [t[:, LANES:] for t in ry]
    gu = [_dot_tn(jnp.concatenate([bh, kh], axis=0),
                  jnp.concatenate([t, jnp.concatenate([jnp.zeros_like(vv), vv], axis=1)], axis=0))
          for bh, kh, t, vv in zip(b_h, k_h, mp, vc)]
    g_mat = [jnp.where(same_head, t[:, :LANES], 0.0) + jnp.where(eye, jnp.exp(cl), 0.0)
             for t, cl in zip(gu, c_last)]
    u_mat = [jnp.where(same_head, t[:, LANES:], 0.0) for t in gu]

    z = [z_ref[p] for p in pairs]
    ys = [[None] * len(chunks) for _ in pairs]
    for c in chunks:
        for p in pairs:
            j = p * len(chunks) + c
            out = _dot(jnp.concatenate([g_mat[j], rq[j]], axis=0), z[p])
            ys[p][c] = out[LANES:] + y0[j]
            z[p] = out[:LANES] + u_mat[j]
    for p in pairs:
        z_ref[p] = z[p]

    y = jnp.concatenate([jnp.concatenate(ys[p], axis=0) for p in pairs], axis=1)
    inv_n = 1.0 / RW_HEAD_DIM
    mean = head_sums(y) * inv_n
    yc = y - mean
    var = head_sums(yc * yc) * inv_n
    y = yc * lax.rsqrt(var + RW_GN_EPS) * lnw_ref[...] + lnb_ref[...]
    o_ref[...] = _bf((y + bonus) * g)


def _rwkv7(rw_p, mu, w0, wwa_bf, a0, g2_bf, k_k, k_a, r_k, ln_w, ln_b, batch, seq):
    m = rw_p.shape[0]
    nt = seq // SCAN_TILE

    def vec(width):
        return pl.BlockSpec((1, width), lambda b, t: (0, 0))

    return pl.pallas_call(
        _rwkv7_kernel,
        grid=(batch, nt),
        in_specs=[
            pl.BlockSpec((SCAN_TILE, RW_COLS), lambda b, t: (b * nt + t, 0)),
            vec(RW_COLS),
            vec(RW_WIDTH),
            pl.BlockSpec((LANES, 2 * RW_WIDTH), lambda b, t: (0, 0)),
            vec(RW_WIDTH),
            pl.BlockSpec((RW_GATE_LORA, RW_WIDTH), lambda b, t: (0, 0)),
            vec(RW_WIDTH), vec(RW_WIDTH), vec(RW_WIDTH), vec(RW_WIDTH), vec(RW_WIDTH),
        ],
        out_specs=pl.BlockSpec((SCAN_TILE, RW_WIDTH), lambda b, t: (b * nt + t, 0)),
        out_shape=jax.ShapeDtypeStruct((m, RW_WIDTH), BF16),
        scratch_shapes=[
            pltpu.VMEM((RW_PAIRS, LANES, LANES), F32),
            pltpu.VMEM((8, RW_COLS), F32),
        ],
        compiler_params=pltpu.CompilerParams(
            dimension_semantics=("arbitrary", "arbitrary"), vmem_limit_bytes=VMEM_LIMIT_BYTES),
        name="rwkv7",
    )(rw_p, mu, w0, wwa_bf, a0, g2_bf, k_k, k_a, r_k, ln_w, ln_b)


def _out_proj_kernel(x_ref, ohg_ref, orw_ref, w_ref, nw_ref, x1_ref, h2_ref):
    x1 = (x_ref[...]
          + jnp.dot(ohg_ref[...], w_ref[:HG_WIDTH], preferred_element_type=F32)
          + jnp.dot(orw_ref[...], w_ref[HG_WIDTH:], preferred_element_type=F32))
    x1_ref[...] = x1
    h2_ref[...] = _bf(x1 * lax.rsqrt(jnp.mean(x1 * x1, axis=-1, keepdims=True) + NORM_EPS) * nw_ref[...])


def _out_proj(x2, o_hg, o_rw, w_out_bf, norm_w):
    m = x2.shape[0]
    return pl.pallas_call(
        _out_proj_kernel,
        grid=(m // ROW_TILE,),
        in_specs=[
            pl.BlockSpec((ROW_TILE, D_MODEL), lambda i: (i, 0)),
            pl.BlockSpec((ROW_TILE, HG_WIDTH), lambda i: (i, 0)),
            pl.BlockSpec((ROW_TILE, RW_WIDTH), lambda i: (i, 0)),
            pl.BlockSpec((D_MODEL, D_MODEL), lambda i: (0, 0)),
            pl.BlockSpec((1, D_MODEL), lambda i: (0, 0)),
        ],
        out_specs=[
            pl.BlockSpec((ROW_TILE, D_MODEL), lambda i: (i, 0)),
            pl.BlockSpec((ROW_TILE, D_MODEL), lambda i: (i, 0)),
        ],
        out_shape=[
            jax.ShapeDtypeStruct((m, D_MODEL), F32),
            jax.ShapeDtypeStruct((m, D_MODEL), BF16),
        ],
        compiler_params=pltpu.CompilerParams(
            dimension_semantics=("arbitrary",), vmem_limit_bytes=VMEM_LIMIT_BYTES),
        name="out_proj",
    )(x2, o_hg, o_rw, w_out_bf, norm_w)


def _ffn_up_kernel(h_ref, w_ref, cw_ref, cb_ref, act_ref, prev_ref, u_ref):
    @pl.when(pl.program_id(1) == 0)
    def _():
        prev_ref[...] = jnp.zeros_like(prev_ref)

    h = h_ref[...]
    n = h.shape[0]

    ct = FFN_COL_TILE
    for j in range(D_FF // ct):
        u = u_ref.at[j % 2]
        for half, cols in enumerate((slice(j * ct, (j + 1) * ct), slice(D_FF + j * ct, D_FF + (j + 1) * ct))):
            dst = slice(half * ct, (half + 1) * ct)
            u[0:8, dst] = prev_ref[:, cols]
            u[8:, dst] = jnp.dot(h, w_ref[:, cols], preferred_element_type=F32)
            prev_ref[:, cols] = u[n:n + 8, dst]
            cw = cw_ref[:, cols]
            c = (cw[2:3] * u[8:n + 8, dst] + cw[1:2] * u[7:n + 7, dst] + cw[0:1] * u[6:n + 6, dst]
                 + cb_ref[:, cols])
            if half == 0:
                gate = c
        act_ref[:, j * ct:(j + 1) * ct] = _bf(_silu(gate) * c)


def _ffn_up(h2, w_up_bf, conv_w, conv_b, batch, seq):
    m = h2.shape[0]
    nt = seq // ROW_TILE
    return pl.pallas_call(
        _ffn_up_kernel,
        grid=(batch, nt),
        in_specs=[
            pl.BlockSpec((ROW_TILE, D_MODEL), lambda b, t: (b * nt + t, 0)),
            pl.BlockSpec((D_MODEL, 2 * D_FF), lambda b, t: (0, 0)),
            pl.BlockSpec((3, 2 * D_FF), lambda b, t: (0, 0)),
            pl.BlockSpec((1, 2 * D_FF), lambda b, t: (0, 0)),
        ],
        out_specs=pl.BlockSpec((ROW_TILE, D_FF), lambda b, t: (b * nt + t, 0)),
        out_shape=jax.ShapeDtypeStruct((m, D_FF), BF16),
        scratch_shapes=[pltpu.VMEM((8, 2 * D_FF), F32),
                        pltpu.VMEM((2, 8 + ROW_TILE, 2 * FFN_COL_TILE), F32)],
        compiler_params=pltpu.CompilerParams(
            dimension_semantics=("arbitrary", "arbitrary"), vmem_limit_bytes=VMEM_LIMIT_BYTES),
        name="ffn_up",
    )(h2, w_up_bf, conv_w, conv_b)


def _ffn_down_kernel(act_ref, w_ref, x1_ref, nw_ref, o_ref):
    x2 = x1_ref[...] + jnp.dot(act_ref[...], w_ref[...], preferred_element_type=F32)
    o_ref[...] = x2 * lax.rsqrt(jnp.mean(x2 * x2, axis=-1, keepdims=True) + NORM_EPS) * nw_ref[...]


def _ffn_down(act, w_down_bf, x1, norm_w):
    m = act.shape[0]
    return pl.pallas_call(
        _ffn_down_kernel,
        grid=(m // ROW_TILE,),
        in_specs=[
            pl.BlockSpec((ROW_TILE, D_FF), lambda i: (i, 0)),
            pl.BlockSpec((D_FF, D_MODEL), lambda i: (0, 0)),
            pl.BlockSpec((ROW_TILE, D_MODEL), lambda i: (i, 0)),
            pl.BlockSpec((1, D_MODEL), lambda i: (0, 0)),
        ],
        out_specs=pl.BlockSpec((ROW_TILE, D_MODEL), lambda i: (i, 0)),
        out_shape=jax.ShapeDtypeStruct((m, D_MODEL), F32),
        compiler_params=pltpu.CompilerParams(
            dimension_semantics=("arbitrary",), vmem_limit_bytes=VMEM_LIMIT_BYTES),
        name="ffn_down",
    )(act, w_down_bf, x1, norm_w)


def kernel(x, norm1_w, w_in, hg_lb_logits, hg_norm_w, rw_shift_mu, rw_w0, rw_w2, rw_a0, rw_a2, rw_g2,
           rw_k_k, rw_k_a, rw_r_k, rw_ln_w, rw_ln_b, w_out, norm2_w, w_up, conv_w, conv_b, w_down,
           final_norm_w):
    batch, seq, _ = x.shape
    assert seq % SCAN_TILE == 0 and seq % ROW_TILE == 0 and norm1_w.shape[0] == 1
    assert CHUNK == RW_HEAD_DIM and 2 * RW_HEAD_DIM == LANES
    x2 = x.reshape(batch * seq, D_MODEL)

    zeros = jnp.zeros_like(rw_w2[0])
    wwa = _bf(jnp.concatenate([jnp.concatenate([rw_w2[0], zeros], axis=1),
                               jnp.concatenate([zeros, rw_a2[0]], axis=1)], axis=0))

    hg_p, rw_p = _in_proj(x2, norm1_w, _bf(w_in[0]))
    o_hg = _hgrn2(hg_p, hg_lb_logits, hg_norm_w, batch, seq)
    o_rw = _rwkv7(rw_p, rw_shift_mu, rw_w0, wwa, rw_a0, _bf(rw_g2[0]), rw_k_k, rw_k_a, rw_r_k,
                  rw_ln_w, rw_ln_b, batch, seq)
    x1, h2 = _out_proj(x2, o_hg, o_rw, _bf(w_out[0]), norm2_w)
    act = _ffn_up(h2, _bf(w_up[0]), conv_w[0], conv_b, batch, seq)
    out = _ffn_down(act, _bf(w_down[0]), x1, final_norm_w.reshape(1, D_MODEL))
    return out.reshape(batch, seq, D_MODEL)
```

```python
import jax
import jax.numpy as jnp
from jax import lax
from jax.experimental import pallas as pl
from jax.experimental.pallas import tpu as pltpu

F32 = jnp.float32
BF16 = jnp.bfloat16

D_MODEL = 1024
HG_WIDTH = 512
HG_HEAD_DIM = 128
HG_HEADS = HG_WIDTH // HG_HEAD_DIM
RW_WIDTH = 512
RW_HEAD_DIM = 64
RW_DECAY_LORA = 64
RW_AAA_LORA = 64
RW_GATE_LORA = 128
RW_COLS = 3 * RW_WIDTH + RW_DECAY_LORA + RW_AAA_LORA + RW_GATE_LORA
HG_COLS = 4 * HG_WIDTH
IN_COLS = HG_COLS + RW_COLS
D_FF = 2816
NORM_EPS = 1e-6
RW_GN_EPS = 64e-5
L2_EPS = 1e-12

LANES = 128
CHUNK = 64
RW_PAIRS = RW_WIDTH // LANES
VMEM_LIMIT_BYTES = 56 * 1024 * 1024

ROW_TILE = 512
SCAN_TILE = 256
FFN_COL_TILE = 256


def _bf(x):
    return x.astype(BF16)


def _dot(a, b):
    return jnp.dot(_bf(a), _bf(b), preferred_element_type=F32)


def _dot_nt(a, b):
    return lax.dot_general(_bf(a), _bf(b), (((1,), (1,)), ((), ())), preferred_element_type=F32)


def _dot_tn(a, b):
    return jnp.dot(_bf(a.T), _bf(b), preferred_element_type=F32)


def _split3(x):
    h1 = _bf(x)
    r1 = x - h1.astype(F32)
    h2 = _bf(r1)
    r2 = r1 - h2.astype(F32)
    return h1, h2, _bf(r2)


def _dot_exact_rhs(m01, x):
    w = x.shape[1]
    out = jnp.dot(_bf(m01), jnp.concatenate(_split3(x), axis=1), preferred_element_type=F32)
    return out[:, :w] + out[:, w:2 * w] + out[:, 2 * w:]


def _dot_exact_lhs(x, m01):
    n = x.shape[0]
    out = jnp.dot(jnp.concatenate(_split3(x), axis=0), _bf(m01), preferred_element_type=F32)
    return out[:n] + out[n:2 * n] + out[2 * n:]


def _iota2(shape, dim):
    return lax.broadcasted_iota(jnp.int32, shape, dim)


def _chunk_tril(n):
    r = _iota2((n, n), 0)
    c = _iota2((n, n), 1)
    return jnp.where((r // CHUNK == c // CHUNK) & (c <= r), 1.0, 0.0).astype(F32)


def _shift_rows(x, prev_rows, k):
    rolled = pltpu.roll(x, k, axis=0)
    prev = pltpu.roll(prev_rows, k, axis=0)
    head = jnp.where(_iota2(prev.shape, 0) < k, prev, rolled[:8])
    return jnp.concatenate([head, rolled[8:]], axis=0)


def _silu(x):
    return x * jax.nn.sigmoid(x)


def _in_proj_kernel(x_ref, nw_ref, w_ref, hg_ref, rw_ref):
    x = x_ref[...]
    h = x * lax.rsqrt(jnp.mean(x * x, axis=-1, keepdims=True) + NORM_EPS) * nw_ref[...]
    hb = _bf(h)
    hg_ref[...] = jnp.dot(hb, w_ref[:, :HG_COLS], preferred_element_type=F32)
    rw_ref[...] = jnp.dot(hb, w_ref[:, HG_COLS:], preferred_element_type=F32)


def _in_proj(x2, norm_w, w_in_bf):
    m = x2.shape[0]
    return pl.pallas_call(
        _in_proj_kernel,
        grid=(m // ROW_TILE,),
        in_specs=[
            pl.BlockSpec((ROW_TILE, D_MODEL), lambda i: (i, 0)),
            pl.BlockSpec((1, D_MODEL), lambda i: (0, 0)),
            pl.BlockSpec((D_MODEL, IN_COLS), lambda i: (0, 0)),
        ],
        out_specs=[
            pl.BlockSpec((ROW_TILE, HG_COLS), lambda i: (i, 0)),
            pl.BlockSpec((ROW_TILE, RW_COLS), lambda i: (i, 0)),
        ],
        out_shape=[
            jax.ShapeDtypeStruct((m, HG_COLS), F32),
            jax.ShapeDtypeStruct((m, RW_COLS), F32),
        ],
        compiler_params=pltpu.CompilerParams(
            dimension_semantics=("arbitrary",), vmem_limit_bytes=VMEM_LIMIT_BYTES),
        name="in_proj",
    )(x2, norm_w, w_in_bf)


def _hgrn2_kernel(x_ref, lbl_ref, nw_ref, o_ref, st_ref):
    @pl.when(pl.program_id(1) == 0)
    def _():
        st_ref[...] = jnp.zeros_like(st_ref)

    logits = lbl_ref[...]
    e = jnp.exp(logits - jnp.max(logits, axis=0, keepdims=True))
    lb = e[0:1] / jnp.sum(e, axis=0, keepdims=True)

    w = HG_WIDTH
    f = lb + (1.0 - lb) * jax.nn.sigmoid(x_ref[:, w:2 * w])
    kk = 1.0 - f
    q = _silu(x_ref[:, :w]) * (HG_HEAD_DIM ** -0.5)
    v = x_ref[:, 2 * w:3 * w]
    n = q.shape[0]
    b = _dot_exact_rhs(_chunk_tril(n), jnp.log2(f))

    heads = range(HG_HEADS)
    chunks = range(n // CHUNK)
    probs = [(h, c) for h in heads for c in chunks]

    def blk(x, h, c):
        return x[c * CHUNK:(c + 1) * CHUNK, h * LANES:(h + 1) * LANES]

    def bcast_rows(x, period, offset):
        return jnp.concatenate(
            [jnp.broadcast_to(x[g * period + offset:g * period + offset + 1], (period, x.shape[1]))
             for g in range(x.shape[0] // period)], axis=0)

    t_idx = _iota2(b.shape, 0)
    srow = _iota2((CHUNK, CHUNK), 0)
    scol = _iota2((CHUNK, CHUNK), 1)
    q_lv, k_lv, sel = [q * f], [kk], []
    size = 2
    while size <= CHUNK:
        half = size // 2
        if size == 4:
            pos = t_idx % size
            m = jnp.where(pos == 0, pltpu.roll(b, n - 1, axis=0),
                          jnp.where(pos == 1, b,
                                    jnp.where(pos == 2, pltpu.roll(b, 1, axis=0), pltpu.roll(b, 2, axis=0))))
        elif size > 4:
            m = bcast_rows(b, size, half - 1)
        if size > 2:
            ez = jnp.exp2(-jnp.abs(b - m))
            q_lv.append(q * ez)
            k_lv.append(kk * ez)
        sel.append((srow // size == scol // size) & (srow % size >= half) & (scol % size < half))
        size *= 2
    qe_all = q * jnp.exp2(b)
    kl_all = kk * jnp.exp2(bcast_rows(b, CHUNK, CHUNK - 1) - b)

    b_last = [blk(b, h, c)[CHUNK - 1:CHUNK] for h, c in probs]
    qe = [blk(qe_all, h, c) for h, c in probs]
    vc = [blk(v, h, c) for h, c in probs]

    s_lv = [[_dot_nt(blk(ql, h, c), blk(kl, h, c)) for h, c in probs] for ql, kl in zip(q_lv, k_lv)]
    a = []
    for j in range(len(probs)):
        acc = jnp.zeros((CHUNK, CHUNK), F32)
        for lv in range(len(sel)):
            acc = jnp.where(sel[lv], s_lv[lv][j], acc)
        a.append(acc)
    u = [_dot_tn(x, blk(kl_all, h, c)) for x, (h, c) in zip(vc, probs)]
    qk = q * kk
    o_intra = [_dot(x, y) + jnp.sum(blk(qk, h, c), axis=-1, keepdims=True) * y
               for x, y, (h, c) in zip(a, vc, probs)]

    st = [st_ref[h] for h in heads]
    outs = [[None] * len(chunks) for _ in heads]
    for c in chunks:
        for h in heads:
            j = h * len(chunks) + c
            outs[h][c] = o_intra[j] + _dot_nt(qe[j], st[h])
            st[h] = st[h] * jnp.exp2(b_last[j]) + u[j]
    for h in heads:
        st_ref[h] = st[h]

    cols = []
    for h in heads:
        o = jnp.concatenate(outs[h], axis=0)
        cols.append(o * lax.rsqrt(jnp.mean(o * o, axis=-1, keepdims=True) + NORM_EPS))
    o = jnp.concatenate(cols, axis=1)
    o_ref[...] = _bf(o * nw_ref[...] * _silu(x_ref[:, 3 * w:]))


def _hgrn2(hg_p, lb_logits, norm_w, batch, seq):
    m = hg_p.shape[0]
    nt = seq // SCAN_TILE
    return pl.pallas_call(
        _hgrn2_kernel,
        grid=(batch, nt),
        in_specs=[
            pl.BlockSpec((SCAN_TILE, HG_COLS), lambda b, t: (b * nt + t, 0)),
            pl.BlockSpec((2, HG_WIDTH), lambda b, t: (0, 0)),
            pl.BlockSpec((1, HG_WIDTH), lambda b, t: (0, 0)),
        ],
        out_specs=pl.BlockSpec((SCAN_TILE, HG_WIDTH), lambda b, t: (b * nt + t, 0)),
        out_shape=jax.ShapeDtypeStruct((m, HG_WIDTH), BF16),
        scratch_shapes=[pltpu.VMEM((HG_HEADS, HG_HEAD_DIM, HG_HEAD_DIM), F32)],
        compiler_params=pltpu.CompilerParams(
            dimension_semantics=("arbitrary", "arbitrary"), vmem_limit_bytes=VMEM_LIMIT_BYTES),
        name="hgrn2",
    )(hg_p, lb_logits, norm_w)


def _stack2(x):
    lane = _iota2(x.shape, 1)
    return jnp.concatenate([jnp.where(lane < RW_HEAD_DIM, x, 0.0),
                            jnp.where(lane >= RW_HEAD_DIM, x, 0.0)], axis=0)


def _rwkv7_kernel(x_ref, mu_ref, w0_ref, wwa_ref, a0_ref, g2_ref, kk_ref, ka_ref, rk_ref, lnw_ref, lnb_ref,
                  o_ref, z_ref, prev_ref):
    @pl.when(pl.program_id(1) == 0)
    def _():
        z_ref[...] = jnp.zeros_like(z_ref)
        prev_ref[...] = jnp.zeros_like(prev_ref)

    x = x_ref[...]
    n = x.shape[0]
    xs = _shift_rows(x, prev_ref[...], 1)
    prev_ref[...] = x[n - 8:]
    x = x + (xs - x) * mu_ref[...]
    w = RW_WIDTH
    r, k, v = x[:, :w], x[:, w:2 * w], x[:, 2 * w:3 * w]
    wa = x[:, 3 * w:3 * w + LANES]
    gd = x[:, 3 * w + LANES:]

    wa = jnp.where(_iota2(wa.shape, 1) < RW_DECAY_LORA, jnp.tanh(wa), wa)
    lora = _dot(wa, wwa_ref[...])
    zw = -(w0_ref[...] + lora[:, :w])
    softplus = jnp.maximum(zw, 0.0) + jnp.log(1.0 + jnp.exp(-jnp.abs(zw)))
    logw = -jnp.exp(-softplus - 0.5)
    a = jax.nn.sigmoid(a0_ref[...] + lora[:, w:])
    g = _dot(jax.nn.sigmoid(gd), g2_ref[...])

    lane = _iota2((LANES, LANES), 1)
    row = _iota2((LANES, LANES), 0)
    same_head = (row // RW_HEAD_DIM) == (lane // RW_HEAD_DIM)
    head_ones = jnp.where(same_head, 1.0, 0.0).astype(F32)

    def head_sums(t):
        return jnp.concatenate(
            [_dot_exact_lhs(t[:, p * LANES:(p + 1) * LANES], head_ones) for p in range(RW_PAIRS)], axis=1)

    kk = k * kk_ref[...]
    kk = kk / jnp.maximum(jnp.sqrt(head_sums(kk * kk)), L2_EPS)
    k = k * (1.0 + (a - 1.0) * ka_ref[...])
    alpha = -kk
    beta = kk * a
    bonus = head_sums(r * k * rk_ref[...]) * v

    cum = _dot_exact_rhs(_chunk_tril(n), logw)
    e_pos = jnp.exp(cum)
    e_neg = jnp.exp(-cum)
    a_t = alpha * jnp.exp(cum - logw)
    r_t = r * e_pos
    b_t = beta * e_neg
    k_t = k * e_neg

    wrow = _iota2((CHUNK, LANES), 0)
    wcol = _iota2((CHUNK, LANES), 1) % CHUNK
    strict = wcol < wrow
    incl = wcol <= wrow
    eye_w = jnp.where(wcol == wrow, 1.0, 0.0).astype(F32)
    eye = lane == row

    pairs = range(RW_PAIRS)
    chunks = range(n // CHUNK)
    probs = [(p, c) for p in pairs for c in chunks]

    def blk(t, p, c):
        return t[c * CHUNK:(c + 1) * CHUNK, p * LANES:(p + 1) * LANES]

    c_last = [blk(cum, p, c)[CHUNK - 1:CHUNK] for p, c in probs]
    e_last = [jnp.exp(cl - blk(cum, p, c)) for (p, c), cl in zip(probs, c_last)]
    a_c = [blk(a_t, p, c) for p, c in probs]
    r_c = [blk(r_t, p, c) for p, c in probs]
    b_h = [blk(beta, p, c) * e for (p, c), e in zip(probs, e_last)]
    k_h = [blk(k, p, c) * e for (p, c), e in zip(probs, e_last)]
    vc = [blk(v, p, c) for p, c in probs]
    v_st = [_stack2(t) for t in vc]

    s4 = [_dot_nt(jnp.concatenate([aa, rr], axis=0),
                  jnp.concatenate([_stack2(blk(b_t, p, c)), _stack2(blk(k_t, p, c))], axis=0))
          for aa, rr, (p, c) in zip(a_c, r_c, probs)]
    l_ab = [jnp.where(strict, s[:CHUNK, :LANES], 0.0) for s in s4]
    l_ak = [jnp.where(strict, s[:CHUNK, LANES:], 0.0) for s in s4]
    a_rb = [jnp.where(incl, s[CHUNK:, :LANES], 0.0) for s in s4]
    a_rk = [jnp.where(incl, s[CHUNK:, LANES:], 0.0) for s in s4]

    tm = [l + eye_w for l in l_ab]
    lp = [_dot(l, _stack2(l)) for l in l_ab]
    w1 = [_dot(l, t) for l, t in zip(l_ak, v_st)]
    for _ in range(4):
        both = [_dot(jnp.concatenate([t, l], axis=0), _stack2(l)) for t, l in zip(tm, lp)]
        tm = [t + bo[:CHUNK] for t, bo in zip(tm, both)]
        lp = [bo[CHUNK:] for bo in both]
    tm = [t + _dot(t, _stack2(l)) for t, l in zip(tm, lp)]

    mp = [_dot(t, jnp.concatenate([_stack2(aa), _stack2(ww)], axis=1))
          for t, aa, ww in zip(tm, a_c, w1)]
    ry = [_dot(jnp.concatenate([lb, lk], axis=1),
               jnp.concatenate([jnp.concatenate([_stack2(t[:, :LANES]), _stack2(t[:, LANES:])], axis=1),
                                jnp.concatenate([jnp.zeros_like(vs), vs], axis=1)], axis=0))
          for lb, lk, t, vs in zip(a_rb, a_rk, mp, v_st)]
    rq = [rr + t[:, :LANES] for rr, t in zip(r_c, ry)]
    y0 = [t[:, LANES:] for t in ry]
    gu = [_dot_tn(jnp.concatenate([bh, kh], axis=0),
                  jnp.concatenate([t, jnp.concatenate([jnp.zeros_like(vv), vv], axis=1)], axis=0))
          for bh, kh, t, vv in zip(b_h, k_h, mp, vc)]
    g_mat = [jnp.where(same_head, t[:, :LANES], 0.0) + jnp.where(eye, jnp.exp(cl), 0.0)
             for t, cl in zip(gu, c_last)]
    u_mat = [jnp.where(same_head, t[:, LANES:], 0.0) for t in gu]

    z = [z_ref[p] for p in pairs]
    ys = [[None] * len(chunks) for _ in pairs]
    for c in chunks:
        for p in pairs:
            j = p * len(chunks) + c
            out = _dot(jnp.concatenate([g_mat[j], rq[j]], axis=0), z[p])
            ys[p][c] = out[LANES:] + y0[j]
            z[p] = out[:LANES] + u_mat[j]
    for p in pairs:
        z_ref[p] = z[p]

    y = jnp.concatenate([jnp.concatenate(ys[p], axis=0) for p in pairs], axis=1)
    inv_n = 1.0 / RW_HEAD_DIM
    mean = head_sums(y) * inv_n
    yc = y - mean
    var = head_sums(yc * yc) * inv_n
    y = yc * lax.rsqrt(var + RW_GN_EPS) * lnw_ref[...] + lnb_ref[...]
    o_ref[...] = _bf((y + bonus) * g)


def _rwkv7(rw_p, mu, w0, wwa_bf, a0, g2_bf, k_k, k_a, r_k, ln_w, ln_b, batch, seq):
    m = rw_p.shape[0]
    nt = seq // SCAN_TILE

    def vec(width):
        return pl.BlockSpec((1, width), lambda b, t: (0, 0))

    return pl.pallas_call(
        _rwkv7_kernel,
        grid=(batch, nt),
        in_specs=[
            pl.BlockSpec((SCAN_TILE, RW_COLS), lambda b, t: (b * nt + t, 0)),
            vec(RW_COLS),
            vec(RW_WIDTH),
            pl.BlockSpec((LANES, 2 * RW_WIDTH), lambda b, t: (0, 0)),
            vec(RW_WIDTH),
            pl.BlockSpec((RW_GATE_LORA, RW_WIDTH), lambda b, t: (0, 0)),
            vec(RW_WIDTH), vec(RW_WIDTH), vec(RW_WIDTH), vec(RW_WIDTH), vec(RW_WIDTH),
        ],
        out_specs=pl.BlockSpec((SCAN_TILE, RW_WIDTH), lambda b, t: (b * nt + t, 0)),
        out_shape=jax.ShapeDtypeStruct((m, RW_WIDTH), BF16),
        scratch_shapes=[
            pltpu.VMEM((RW_PAIRS, LANES, LANES), F32),
            pltpu.VMEM((8, RW_COLS), F32),
        ],
        compiler_params=pltpu.CompilerParams(
            dimension_semantics=("arbitrary", "arbitrary"), vmem_limit_bytes=VMEM_LIMIT_BYTES),
        name="rwkv7",
    )(rw_p, mu, w0, wwa_bf, a0, g2_bf, k_k, k_a, r_k, ln_w, ln_b)


def _rmsnorm(x, w):
    return x * lax.rsqrt(jnp.mean(x * x, axis=-1, keepdims=True) + NORM_EPS) * w


def _mix_ffn_kernel(x_ref, ohg_ref, orw_ref, wo_ref, n2_ref, wu_ref, cw_ref, cb_ref, wd_ref, nf_ref,
                    o_ref, prev_ref, u_ref, act_ref):
    @pl.when(pl.program_id(1) == 0)
    def _():
        prev_ref[...] = jnp.zeros_like(prev_ref)

    x1 = (x_ref[...]
          + jnp.dot(ohg_ref[...], wo_ref[:HG_WIDTH], preferred_element_type=F32)
          + jnp.dot(orw_ref[...], wo_ref[HG_WIDTH:], preferred_element_type=F32))
    h = _bf(_rmsnorm(x1, n2_ref[...]))
    n = h.shape[0]

    ct = FFN_COL_TILE
    ntile = D_FF // ct

    def halves(j):
        return ((slice(0, ct), slice(j * ct, (j + 1) * ct)),
                (slice(ct, 2 * ct), slice(D_FF + j * ct, D_FF + (j + 1) * ct)))

    def up(j):
        u = u_ref.at[j % 2]
        for dst, cols in halves(j):
            u[0:8, dst] = prev_ref[:, cols]
            u[8:, dst] = jnp.dot(h, wu_ref[:, cols], preferred_element_type=F32)
            prev_ref[:, cols] = u[n:n + 8, dst]

    def conv_gate(j):
        u = u_ref.at[j % 2]
        c = []
        for dst, cols in halves(j):
            cw = cw_ref[:, cols]
            c.append(cw[2:3] * u[8:n + 8, dst] + cw[1:2] * u[7:n + 7, dst] + cw[0:1] * u[6:n + 6, dst]
                     + cb_ref[:, cols])
        act_ref[:, j * ct:(j + 1) * ct] = _bf(_silu(c[0]) * c[1])

    up(0)
    for j in range(ntile):
        if j + 1 < ntile:
            up(j + 1)
        conv_gate(j)
    y = x1 + jnp.dot(act_ref[...], wd_ref[...], preferred_element_type=F32)
    o_ref[...] = _rmsnorm(y, nf_ref[...])


def _mix_ffn(x2, o_hg, o_rw, w_out_bf, norm2_w, w_up_bf, conv_w, conv_b, w_down_bf, final_w, batch, seq):
    m = x2.shape[0]
    nt = seq // ROW_TILE

    def rows(width):
        return pl.BlockSpec((ROW_TILE, width), lambda b, t: (b * nt + t, 0))

    def whole(shape):
        return pl.BlockSpec(shape, lambda b, t: (0, 0), pipeline_mode=pl.Buffered(1))

    return pl.pallas_call(
        _mix_ffn_kernel,
        grid=(batch, nt),
        in_specs=[
            rows(D_MODEL), rows(HG_WIDTH), rows(RW_WIDTH),
            whole((D_MODEL, D_MODEL)), whole((1, D_MODEL)),
            whole((D_MODEL, 2 * D_FF)), whole((3, 2 * D_FF)), whole((1, 2 * D_FF)),
            whole((D_FF, D_MODEL)), whole((1, D_MODEL)),
        ],
        out_specs=rows(D_MODEL),
        out_shape=jax.ShapeDtypeStruct((m, D_MODEL), F32),
        scratch_shapes=[pltpu.VMEM((8, 2 * D_FF), F32),
                        pltpu.VMEM((2, 8 + ROW_TILE, 2 * FFN_COL_TILE), F32),
                        pltpu.VMEM((ROW_TILE, D_FF), BF16)],
        compiler_params=pltpu.CompilerParams(
            dimension_semantics=("arbitrary", "arbitrary"), vmem_limit_bytes=VMEM_LIMIT_BYTES),
        name="mix_ffn",
    )(x2, o_hg, o_rw, w_out_bf, norm2_w, w_up_bf, conv_w, conv_b, w_down_bf, final_w)


def kernel(x, norm1_w, w_in, hg_lb_logits, hg_norm_w, rw_shift_mu, rw_w0, rw_w2, rw_a0, rw_a2, rw_g2,
           rw_k_k, rw_k_a, rw_r_k, rw_ln_w, rw_ln_b, w_out, norm2_w, w_up, conv_w, conv_b, w_down,
           final_norm_w):
    batch, seq, _ = x.shape
    assert seq % SCAN_TILE == 0 and seq % ROW_TILE == 0 and norm1_w.shape[0] == 1
    assert CHUNK == RW_HEAD_DIM and 2 * RW_HEAD_DIM == LANES
    x2 = x.reshape(batch * seq, D_MODEL)

    zeros = jnp.zeros_like(rw_w2[0])
    wwa = _bf(jnp.concatenate([jnp.concatenate([rw_w2[0], zeros], axis=1),
                               jnp.concatenate([zeros, rw_a2[0]], axis=1)], axis=0))

    hg_p, rw_p = _in_proj(x2, norm1_w, _bf(w_in[0]))
    o_hg = _hgrn2(hg_p, hg_lb_logits, hg_norm_w, batch, seq)
    o_rw = _rwkv7(rw_p, rw_shift_mu, rw_w0, wwa, rw_a0, _bf(rw_g2[0]), rw_k_k, rw_k_a, rw_r_k,
                  rw_ln_w, rw_ln_b, batch, seq)
    out = _mix_ffn(x2, o_hg, o_rw, _bf(w_out[0]), norm2_w, _bf(w_up[0]), conv_w[0], conv_b, _bf(w_down[0]),
                   final_norm_w.reshape(1, D_MODEL), batch, seq)
    return out.reshape(batch, seq, D_MODEL)
```

```python
import jax
import jax.numpy as jnp
from jax import lax
from jax.experimental import pallas as pl
from jax.experimental.pallas import tpu as pltpu

F32 = jnp.float32
BF16 = jnp.bfloat16

D_MODEL = 1024
HG_WIDTH = 512
HG_HEAD_DIM = 128
HG_HEADS = HG_WIDTH // HG_HEAD_DIM
RW_WIDTH = 512
RW_HEAD_DIM = 64
RW_DECAY_LORA = 64
RW_AAA_LORA = 64
RW_GATE_LORA = 128
RW_COLS = 3 * RW_WIDTH + RW_DECAY_LORA + RW_AAA_LORA + RW_GATE_LORA
HG_COLS = 4 * HG_WIDTH
IN_COLS = HG_COLS + RW_COLS
D_FF = 2816
NORM_EPS = 1e-6
RW_GN_EPS = 64e-5
L2_EPS = 1e-12

LANES = 128
CHUNK = 64
RW_PAIRS = RW_WIDTH // LANES
VMEM_LIMIT_BYTES = 56 * 1024 * 1024

ROW_TILE = 512
SCAN_TILE = 256
FFN_COL_TILE = 256


def _bf(x):
    return x.astype(BF16)


def _dot(a, b):
    return jnp.dot(_bf(a), _bf(b), preferred_element_type=F32)


def _dot_nt(a, b):
    return lax.dot_general(_bf(a), _bf(b), (((1,), (1,)), ((), ())), preferred_element_type=F32)


def _dot_tn(a, b):
    return jnp.dot(_bf(a.T), _bf(b), preferred_element_type=F32)


def _split3(x):
    h1 = _bf(x)
    r1 = x - h1.astype(F32)
    h2 = _bf(r1)
    r2 = r1 - h2.astype(F32)
    return h1, h2, _bf(r2)


def _dot_exact_rhs(m01, x):
    w = x.shape[1]
    out = jnp.dot(_bf(m01), jnp.concatenate(_split3(x), axis=1), preferred_element_type=F32)
    return out[:, :w] + out[:, w:2 * w] + out[:, 2 * w:]


def _dot_exact_lhs(x, m01):
    n = x.shape[0]
    out = jnp.dot(jnp.concatenate(_split3(x), axis=0), _bf(m01), preferred_element_type=F32)
    return out[:n] + out[n:2 * n] + out[2 * n:]


def _iota2(shape, dim):
    return lax.broadcasted_iota(jnp.int32, shape, dim)


def _chunk_tril(n):
    r = _iota2((n, n), 0)
    c = _iota2((n, n), 1)
    return jnp.where((r // CHUNK == c // CHUNK) & (c <= r), 1.0, 0.0).astype(F32)


def _shift_rows(x, prev_rows, k):
    rolled = pltpu.roll(x, k, axis=0)
    prev = pltpu.roll(prev_rows, k, axis=0)
    head = jnp.where(_iota2(prev.shape, 0) < k, prev, rolled[:8])
    return jnp.concatenate([head, rolled[8:]], axis=0)


def _silu(x):
    return x * jax.nn.sigmoid(x)


def _rmsnorm(x, w):
    return x * lax.rsqrt(jnp.mean(x * x, axis=-1, keepdims=True) + NORM_EPS) * w


def _hgrn2_rows(x, lb, norm_w, st):
    w = HG_WIDTH
    f = lb + (1.0 - lb) * jax.nn.sigmoid(x[:, w:2 * w])
    kk = 1.0 - f
    q = _silu(x[:, :w]) * (HG_HEAD_DIM ** -0.5)
    v = x[:, 2 * w:3 * w]
    n = q.shape[0]
    b = _dot_exact_rhs(_chunk_tril(n), jnp.log2(f))

    heads = range(HG_HEADS)
    chunks = range(n // CHUNK)
    probs = [(h, c) for h in heads for c in chunks]

    def blk(x, h, c):
        return x[c * CHUNK:(c + 1) * CHUNK, h * LANES:(h + 1) * LANES]

    def bcast_rows(x, period, offset):
        return jnp.concatenate(
            [jnp.broadcast_to(x[g * period + offset:g * period + offset + 1], (period, x.shape[1]))
             for g in range(x.shape[0] // period)], axis=0)

    t_idx = _iota2(b.shape, 0)
    srow = _iota2((CHUNK, CHUNK), 0)
    scol = _iota2((CHUNK, CHUNK), 1)
    q_lv, k_lv, sel = [q * f], [kk], []
    size = 2
    while size <= CHUNK:
        half = size // 2
        if size == 4:
            pos = t_idx % size
            m = jnp.where(pos == 0, pltpu.roll(b, n - 1, axis=0),
                          jnp.where(pos == 1, b,
                                    jnp.where(pos == 2, pltpu.roll(b, 1, axis=0), pltpu.roll(b, 2, axis=0))))
        elif size > 4:
            m = bcast_rows(b, size, half - 1)
        if size > 2:
            ez = jnp.exp2(-jnp.abs(b - m))
            q_lv.append(q * ez)
            k_lv.append(kk * ez)
        sel.append((srow // size == scol // size) & (srow % size >= half) & (scol % size < half))
        size *= 2
    qe_all = q * jnp.exp2(b)
    kl_all = kk * jnp.exp2(bcast_rows(b, CHUNK, CHUNK - 1) - b)

    b_last = [blk(b, h, c)[CHUNK - 1:CHUNK] for h, c in probs]
    qe = [blk(qe_all, h, c) for h, c in probs]
    vc = [blk(v, h, c) for h, c in probs]

    s_lv = [[_dot_nt(blk(ql, h, c), blk(kl, h, c)) for h, c in probs] for ql, kl in zip(q_lv, k_lv)]
    a = []
    for j in range(len(probs)):
        acc = jnp.zeros((CHUNK, CHUNK), F32)
        for lv in range(len(sel)):
            acc = jnp.where(sel[lv], s_lv[lv][j], acc)
        a.append(acc)
    u = [_dot_tn(x, blk(kl_all, h, c)) for x, (h, c) in zip(vc, probs)]
    qk = q * kk
    o_intra = [_dot(x, y) + jnp.sum(blk(qk, h, c), axis=-1, keepdims=True) * y
               for x, y, (h, c) in zip(a, vc, probs)]

    st = list(st)
    outs = [[None] * len(chunks) for _ in heads]
    for c in chunks:
        for h in heads:
            j = h * len(chunks) + c
            outs[h][c] = o_intra[j] + _dot_nt(qe[j], st[h])
            st[h] = st[h] * jnp.exp2(b_last[j]) + u[j]

    cols = []
    for h in heads:
        o = jnp.concatenate(outs[h], axis=0)
        cols.append(o * lax.rsqrt(jnp.mean(o * o, axis=-1, keepdims=True) + NORM_EPS))
    o = jnp.concatenate(cols, axis=1)
    return _bf(o * norm_w * _silu(x[:, 3 * w:])), st


def _in_hgrn2_kernel(x_ref, nw_ref, w_ref, lbl_ref, hnw_ref, ohg_ref, rw_ref, st_ref):
    @pl.when(pl.program_id(1) == 0)
    def _():
        st_ref[...] = jnp.zeros_like(st_ref)

    hb = _bf(_rmsnorm(x_ref[...], nw_ref[...]))
    hg = jnp.dot(hb, w_ref[:, :HG_COLS], preferred_element_type=F32)
    rw_ref[...] = jnp.dot(hb, w_ref[:, HG_COLS:], preferred_element_type=F32)

    logits = lbl_ref[...]
    e = jnp.exp(logits - jnp.max(logits, axis=0, keepdims=True))
    lb = e[0:1] / jnp.sum(e, axis=0, keepdims=True)

    st = [st_ref[h] for h in range(HG_HEADS)]
    for i in range(ROW_TILE // SCAN_TILE):
        rows = slice(i * SCAN_TILE, (i + 1) * SCAN_TILE)
        ohg_ref[rows, :], st = _hgrn2_rows(hg[rows], lb, hnw_ref[...], st)
    for h in range(HG_HEADS):
        st_ref[h] = st[h]


def _in_hgrn2(x2, norm_w, w_in_bf, lb_logits, hg_norm_w, batch, seq):
    m = x2.shape[0]
    nt = seq // ROW_TILE

    def rows(width):
        return pl.BlockSpec((ROW_TILE, width), lambda b, t: (b * nt + t, 0))

    def whole(shape):
        return pl.BlockSpec(shape, lambda b, t: (0, 0), pipeline_mode=pl.Buffered(1))

    return pl.pallas_call(
        _in_hgrn2_kernel,
        grid=(batch, nt),
        in_specs=[rows(D_MODEL), whole((1, D_MODEL)), whole((D_MODEL, IN_COLS)),
                  whole((2, HG_WIDTH)), whole((1, HG_WIDTH))],
        out_specs=[rows(HG_WIDTH), rows(RW_COLS)],
        out_shape=[
            jax.ShapeDtypeStruct((m, HG_WIDTH), BF16),
            jax.ShapeDtypeStruct((m, RW_COLS), F32),
        ],
        scratch_shapes=[pltpu.VMEM((HG_HEADS, HG_HEAD_DIM, HG_HEAD_DIM), F32)],
        compiler_params=pltpu.CompilerParams(
            dimension_semantics=("arbitrary", "arbitrary"), vmem_limit_bytes=VMEM_LIMIT_BYTES),
        name="in_hgrn2",
    )(x2, norm_w, w_in_bf, lb_logits, hg_norm_w)


def _stack2(x):
    lane = _iota2(x.shape, 1)
    return jnp.concatenate([jnp.where(lane < RW_HEAD_DIM, x, 0.0),
                            jnp.where(lane >= RW_HEAD_DIM, x, 0.0)], axis=0)


def _rwkv7_kernel(x_ref, mu_ref, w0_ref, wwa_ref, a0_ref, g2_ref, kk_ref, ka_ref, rk_ref, lnw_ref, lnb_ref,
                  o_ref, z_ref, prev_ref):
    @pl.when(pl.program_id(1) == 0)
    def _():
        z_ref[...] = jnp.zeros_like(z_ref)
        prev_ref[...] = jnp.zeros_like(prev_ref)

    x = x_ref[...]
    n = x.shape[0]
    xs = _shift_rows(x, prev_ref[...], 1)
    prev_ref[...] = x[n - 8:]
    x = x + (xs - x) * mu_ref[...]
    w = RW_WIDTH
    r, k, v = x[:, :w], x[:, w:2 * w], x[:, 2 * w:3 * w]
    wa = x[:, 3 * w:3 * w + LANES]
    gd = x[:, 3 * w + LANES:]

    wa = jnp.where(_iota2(wa.shape, 1) < RW_DECAY_LORA, jnp.tanh(wa), wa)
    lora = _dot(wa, wwa_ref[...])
    zw = -(w0_ref[...] + lora[:, :w])
    softplus = jnp.maximum(zw, 0.0) + jnp.log(1.0 + jnp.exp(-jnp.abs(zw)))
    logw = -jnp.exp(-softplus - 0.5)
    a = jax.nn.sigmoid(a0_ref[...] + lora[:, w:])
    g = _dot(jax.nn.sigmoid(gd), g2_ref[...])

    lane = _iota2((LANES, LANES), 1)
    row = _iota2((LANES, LANES), 0)
    same_head = (row // RW_HEAD_DIM) == (lane // RW_HEAD_DIM)
    head_ones = jnp.where(same_head, 1.0, 0.0).astype(F32)

    def head_sums(t):
        return jnp.concatenate(
            [_dot_exact_lhs(t[:, p * LANES:(p + 1) * LANES], head_ones) for p in range(RW_PAIRS)], axis=1)

    kk = k * kk_ref[...]
    kk = kk / jnp.maximum(jnp.sqrt(head_sums(kk * kk)), L2_EPS)
    k = k * (1.0 + (a - 1.0) * ka_ref[...])
    alpha = -kk
    beta = kk * a
    bonus = head_sums(r * k * rk_ref[...]) * v

    cum = _dot_exact_rhs(_chunk_tril(n), logw)
    e_pos = jnp.exp(cum)
    e_neg = jnp.exp(-cum)
    a_t = alpha * jnp.exp(cum - logw)
    r_t = r * e_pos
    b_t = beta * e_neg
    k_t = k * e_neg

    wrow = _iota2((CHUNK, LANES), 0)
    wcol = _iota2((CHUNK, LANES), 1) % CHUNK
    strict = wcol < wrow
    incl = wcol <= wrow
    eye_w = jnp.where(wcol == wrow, 1.0, 0.0).astype(F32)
    eye = lane == row

    pairs = range(RW_PAIRS)
    chunks = range(n // CHUNK)
    probs = [(p, c) for p in pairs for c in chunks]

    def blk(t, p, c):
        return t[c * CHUNK:(c + 1) * CHUNK, p * LANES:(p + 1) * LANES]

    c_last = [blk(cum, p, c)[CHUNK - 1:CHUNK] for p, c in probs]
    e_last = [jnp.exp(cl - blk(cum, p, c)) for (p, c), cl in zip(probs, c_last)]
    a_c = [blk(a_t, p, c) for p, c in probs]
    r_c = [blk(r_t, p, c) for p, c in probs]
    b_h = [blk(beta, p, c) * e for (p, c), e in zip(probs, e_last)]
    k_h = [blk(k, p, c) * e for (p, c), e in zip(probs, e_last)]
    vc = [blk(v, p, c) for p, c in probs]
    v_st = [_stack2(t) for t in vc]

    s4 = [_dot_nt(jnp.concatenate([aa, rr], axis=0),
                  jnp.concatenate([_stack2(blk(b_t, p, c)), _stack2(blk(k_t, p, c))], axis=0))
          for aa, rr, (p, c) in zip(a_c, r_c, probs)]
    l_ab = [jnp.where(strict, s[:CHUNK, :LANES], 0.0) for s in s4]
    l_ak = [jnp.where(strict, s[:CHUNK, LANES:], 0.0) for s in s4]
    a_rb = [jnp.where(incl, s[CHUNK:, :LANES], 0.0) for s in s4]
    a_rk = [jnp.where(incl, s[CHUNK:, LANES:], 0.0) for s in s4]

    tm = [l + eye_w for l in l_ab]
    lp = [_dot(l, _stack2(l)) for l in l_ab]
    w1 = [_dot(l, t) for l, t in zip(l_ak, v_st)]
    for _ in range(4):
        both = [_dot(jnp.concatenate([t, l], axis=0), _stack2(l)) for t, l in zip(tm, lp)]
        tm = [t + bo[:CHUNK] for t, bo in zip(tm, both)]
        lp = [bo[CHUNK:] for bo in both]
    tm = [t + _dot(t, _stack2(l)) for t, l in zip(tm, lp)]

    mp = [_dot(t, jnp.concatenate([_stack2(aa), _stack2(ww)], axis=1))
          for t, aa, ww in zip(tm, a_c, w1)]
    ry = [_dot(jnp.concatenate([lb, lk], axis=1),
               jnp.concatenate([jnp.concatenate([_stack2(t[:, :LANES]), _stack2(t[:, LANES:])], axis=1),
                                jnp.concatenate([jnp.zeros_like(vs), vs], axis=1)], axis=0))
          for lb, lk, t, vs in zip(a_rb, a_rk, mp, v_st)]
    rq = [rr + t[:, :LANES] for rr, t in zip(r_c, ry)]
    y0 = [t[:, LANES:] for t in ry]
    gu = [_dot_tn(jnp.concatenate([bh, kh], axis=0),
                  jnp.concatenate([t, jnp.concatenate([jnp.zeros_like(vv), vv], axis=1)], axis=0))
          for bh, kh, t, vv in zip(b_h, k_h, mp, vc)]
    g_mat = [jnp.where(same_head, t[:, :LANES], 0.0) + jnp.where(eye, jnp.exp(cl), 0.0)
             for t, cl in zip(gu, c_last)]
    u_mat = [jnp.where(same_head, t[:, LANES:], 0.0) for t in gu]

    z = [z_ref[p] for p in pairs]
    ys = [[None] * len(chunks) for _ in pairs]
    for c in chunks:
        for p in pairs:
            j = p * len(chunks) + c
            out = _dot(jnp.concatenate([g_mat[j], rq[j]], axis=0), z[p])
            ys[p][c] = out[LANES:] + y0[j]
            z[p] = out[:LANES] + u_mat[j]
    for p in pairs:
        z_ref[p] = z[p]

    y = jnp.concatenate([jnp.concatenate(ys[p], axis=0) for p in pairs], axis=1)
    inv_n = 1.0 / RW_HEAD_DIM
    mean = head_sums(y) * inv_n
    yc = y - mean
    var = head_sums(yc * yc) * inv_n
    y = yc * lax.rsqrt(var + RW_GN_EPS) * lnw_ref[...] + lnb_ref[...]
    o_ref[...] = _bf((y + bonus) * g)


def _rwkv7(rw_p, mu, w0, wwa_bf, a0, g2_bf, k_k, k_a, r_k, ln_w, ln_b, batch, seq):
    m = rw_p.shape[0]
    nt = seq // SCAN_TILE

    def vec(width):
        return pl.BlockSpec((1, width), lambda b, t: (0, 0))

    return pl.pallas_call(
        _rwkv7_kernel,
        grid=(batch, nt),
        in_specs=[
            pl.BlockSpec((SCAN_TILE, RW_COLS), lambda b, t: (b * nt + t, 0)),
            vec(RW_COLS),
            vec(RW_WIDTH),
            pl.BlockSpec((LANES, 2 * RW_WIDTH), lambda b, t: (0, 0)),
            vec(RW_WIDTH),
            pl.BlockSpec((RW_GATE_LORA, RW_WIDTH), lambda b, t: (0, 0)),
            vec(RW_WIDTH), vec(RW_WIDTH), vec(RW_WIDTH), vec(RW_WIDTH), vec(RW_WIDTH),
        ],
        out_specs=pl.BlockSpec((SCAN_TILE, RW_WIDTH), lambda b, t: (b * nt + t, 0)),
        out_shape=jax.ShapeDtypeStruct((m, RW_WIDTH), BF16),
        scratch_shapes=[
            pltpu.VMEM((RW_PAIRS, LANES, LANES), F32),
            pltpu.VMEM((8, RW_COLS), F32),
        ],
        compiler_params=pltpu.CompilerParams(
            dimension_semantics=("arbitrary", "arbitrary"), vmem_limit_bytes=VMEM_LIMIT_BYTES),
        name="rwkv7",
    )(rw_p, mu, w0, wwa_bf, a0, g2_bf, k_k, k_a, r_k, ln_w, ln_b)


def _mix_ffn_kernel(x_ref, ohg_ref, orw_ref, wo_ref, n2_ref, wu_ref, cw_ref, cb_ref, wd_ref, nf_ref,
                    o_ref, prev_ref, u_ref, act_ref):
    @pl.when(pl.program_id(1) == 0)
    def _():
        prev_ref[...] = jnp.zeros_like(prev_ref)

    x1 = (x_ref[...]
          + jnp.dot(ohg_ref[...], wo_ref[:HG_WIDTH], preferred_element_type=F32)
          + jnp.dot(orw_ref[...], wo_ref[HG_WIDTH:], preferred_element_type=F32))
    h = _bf(_rmsnorm(x1, n2_ref[...]))
    n = h.shape[0]

    ct = FFN_COL_TILE
    ntile = D_FF // ct

    def halves(j):
        return ((slice(0, ct), slice(j * ct, (j + 1) * ct)),
                (slice(ct, 2 * ct), slice(D_FF + j * ct, D_FF + (j + 1) * ct)))

    def up(j):
        u = u_ref.at[j % 2]
        for dst, cols in halves(j):
            u[0:8, dst] = prev_ref[:, cols]
            u[8:, dst] = jnp.dot(h, wu_ref[:, cols], preferred_element_type=F32)
            prev_ref[:, cols] = u[n:n + 8, dst]

    def conv_gate(j):
        u = u_ref.at[j % 2]
        c = []
        for dst, cols in halves(j):
            cw = cw_ref[:, cols]
            c.append(cw[2:3] * u[8:n + 8, dst] + cw[1:2] * u[7:n + 7, dst] + cw[0:1] * u[6:n + 6, dst]
                     + cb_ref[:, cols])
        act_ref[:, j * ct:(j + 1) * ct] = _bf(_silu(c[0]) * c[1])

    up(0)
    for j in range(ntile):
        if j + 1 < ntile:
            up(j + 1)
        conv_gate(j)
    y = x1 + jnp.dot(act_ref[...], wd_ref[...], preferred_element_type=F32)
    o_ref[...] = _rmsnorm(y, nf_ref[...])


def _mix_ffn(x2, o_hg, o_rw, w_out_bf, norm2_w, w_up_bf, conv_w, conv_b, w_down_bf, final_w, batch, seq):
    m = x2.shape[0]
    nt = seq // ROW_TILE

    def rows(width):
        return pl.BlockSpec((ROW_TILE, width), lambda b, t: (b * nt + t, 0))

    def whole(shape):
        return pl.BlockSpec(shape, lambda b, t: (0, 0), pipeline_mode=pl.Buffered(1))

    return pl.pallas_call(
        _mix_ffn_kernel,
        grid=(batch, nt),
        in_specs=[
            rows(D_MODEL), rows(HG_WIDTH), rows(RW_WIDTH),
            whole((D_MODEL, D_MODEL)), whole((1, D_MODEL)),
            whole((D_MODEL, 2 * D_FF)), whole((3, 2 * D_FF)), whole((1, 2 * D_FF)),
            whole((D_FF, D_MODEL)), whole((1, D_MODEL)),
        ],
        out_specs=rows(D_MODEL),
        out_shape=jax.ShapeDtypeStruct((m, D_MODEL), F32),
        scratch_shapes=[pltpu.VMEM((8, 2 * D_FF), F32),
                        pltpu.VMEM((2, 8 + ROW_TILE, 2 * FFN_COL_TILE), F32),
                        pltpu.VMEM((ROW_TILE, D_FF), BF16)],
        compiler_params=pltpu.CompilerParams(
            dimension_semantics=("arbitrary", "arbitrary"), vmem_limit_bytes=VMEM_LIMIT_BYTES),
        name="mix_ffn",
    )(x2, o_hg, o_rw, w_out_bf, norm2_w, w_up_bf, conv_w, conv_b, w_down_bf, final_w)


def kernel(x, norm1_w, w_in, hg_lb_logits, hg_norm_w, rw_shift_mu, rw_w0, rw_w2, rw_a0, rw_a2, rw_g2,
           rw_k_k, rw_k_a, rw_r_k, rw_ln_w, rw_ln_b, w_out, norm2_w, w_up, conv_w, conv_b, w_down,
           final_norm_w):
    batch, seq, _ = x.shape
    assert seq % SCAN_TILE == 0 and seq % ROW_TILE == 0 and norm1_w.shape[0] == 1
    assert CHUNK == RW_HEAD_DIM and 2 * RW_HEAD_DIM == LANES
    x2 = x.reshape(batch * seq, D_MODEL)

    zeros = jnp.zeros_like(rw_w2[0])
    wwa = _bf(jnp.concatenate([jnp.concatenate([rw_w2[0], zeros], axis=1),
                               jnp.concatenate([zeros, rw_a2[0]], axis=1)], axis=0))

    o_hg, rw_p = _in_hgrn2(x2, norm1_w, _bf(w_in[0]), hg_lb_logits, hg_norm_w, batch, seq)
    o_rw = _rwkv7(rw_p, rw_shift_mu, rw_w0, wwa, rw_a0, _bf(rw_g2[0]), rw_k_k, rw_k_a, rw_r_k,
                  rw_ln_w, rw_ln_b, batch, seq)
    out = _mix_ffn(x2, o_hg, o_rw, _bf(w_out[0]), norm2_w, _bf(w_up[0]), conv_w[0], conv_b, _bf(w_down[0]),
                   final_norm_w.reshape(1, D_MODEL), batch, seq)
    return out.reshape(batch, seq, D_MODEL)
```

```python
import jax
import jax.numpy as jnp
from jax import lax
from jax.experimental import pallas as pl
from jax.experimental.pallas import tpu as pltpu

F32 = jnp.float32
BF16 = jnp.bfloat16

D_MODEL = 1024
HG_WIDTH = 512
HG_HEAD_DIM = 128
HG_HEADS = HG_WIDTH // HG_HEAD_DIM
RW_WIDTH = 512
RW_HEAD_DIM = 64
RW_DECAY_LORA = 64
RW_AAA_LORA = 64
RW_GATE_LORA = 128
RW_COLS = 3 * RW_WIDTH + RW_DECAY_LORA + RW_AAA_LORA + RW_GATE_LORA
HG_COLS = 4 * HG_WIDTH
IN_COLS = HG_COLS + RW_COLS
D_FF = 2816
NORM_EPS = 1e-6
RW_GN_EPS = 64e-5
L2_EPS = 1e-12

LANES = 128
CHUNK = 64
RW_PAIRS = RW_WIDTH // LANES
VMEM_LIMIT_BYTES = 56 * 1024 * 1024

ROW_TILE = 512
SCAN_TILE = 256
FFN_COL_TILE = 256


def _bf(x):
    return x.astype(BF16)


def _dot(a, b):
    return jnp.dot(_bf(a), _bf(b), preferred_element_type=F32)


def _dot_nt(a, b):
    return lax.dot_general(_bf(a), _bf(b), (((1,), (1,)), ((), ())), preferred_element_type=F32)


def _dot_tn(a, b):
    return jnp.dot(_bf(a.T), _bf(b), preferred_element_type=F32)


def _split3(x):
    h1 = _bf(x)
    r1 = x - h1.astype(F32)
    h2 = _bf(r1)
    r2 = r1 - h2.astype(F32)
    return h1, h2, _bf(r2)


def _dot_exact_rhs(m01, x):
    w = x.shape[1]
    out = jnp.dot(_bf(m01), jnp.concatenate(_split3(x), axis=1), preferred_element_type=F32)
    return out[:, :w] + out[:, w:2 * w] + out[:, 2 * w:]


def _dot_exact_lhs(x, m01):
    n = x.shape[0]
    out = jnp.dot(jnp.concatenate(_split3(x), axis=0), _bf(m01), preferred_element_type=F32)
    return out[:n] + out[n:2 * n] + out[2 * n:]


def _iota2(shape, dim):
    return lax.broadcasted_iota(jnp.int32, shape, dim)


def _chunk_tril(n):
    r = _iota2((n, n), 0)
    c = _iota2((n, n), 1)
    return jnp.where((r // CHUNK == c // CHUNK) & (c <= r), 1.0, 0.0).astype(F32)


def _shift_rows(x, prev_rows, k):
    rolled = pltpu.roll(x, k, axis=0)
    prev = pltpu.roll(prev_rows, k, axis=0)
    head = jnp.where(_iota2(prev.shape, 0) < k, prev, rolled[:8])
    return jnp.concatenate([head, rolled[8:]], axis=0)


def _silu(x):
    return x * jax.nn.sigmoid(x)


def _rmsnorm(x, w):
    return x * lax.rsqrt(jnp.mean(x * x, axis=-1, keepdims=True) + NORM_EPS) * w


def _hgrn2_rows(x, lb, norm_w, st):
    w = HG_WIDTH
    f = lb + (1.0 - lb) * jax.nn.sigmoid(x[:, w:2 * w])
    kk = 1.0 - f
    q = _silu(x[:, :w]) * (HG_HEAD_DIM ** -0.5)
    v = x[:, 2 * w:3 * w]
    n = q.shape[0]
    b = _dot_exact_rhs(_chunk_tril(n), jnp.log2(f))

    heads = range(HG_HEADS)
    chunks = range(n // CHUNK)
    probs = [(h, c) for h in heads for c in chunks]

    def blk(x, h, c):
        return x[c * CHUNK:(c + 1) * CHUNK, h * LANES:(h + 1) * LANES]

    def bcast_rows(x, period, offset):
        return jnp.concatenate(
            [jnp.broadcast_to(x[g * period + offset:g * period + offset + 1], (period, x.shape[1]))
             for g in range(x.shape[0] // period)], axis=0)

    t_idx = _iota2(b.shape, 0)
    srow = _iota2((CHUNK, CHUNK), 0)
    scol = _iota2((CHUNK, CHUNK), 1)
    q_lv, k_lv, sel = [q * f], [kk], []
    size = 2
    while size <= CHUNK:
        half = size // 2
        if size == 4:
            pos = t_idx % size
            m = jnp.where(pos == 0, pltpu.roll(b, n - 1, axis=0),
                          jnp.where(pos == 1, b,
                                    jnp.where(pos == 2, pltpu.roll(b, 1, axis=0), pltpu.roll(b, 2, axis=0))))
        elif size > 4:
            m = bcast_rows(b, size, half - 1)
        if size > 2:
            ez = jnp.exp2(-jnp.abs(b - m))
            q_lv.append(q * ez)
            k_lv.append(kk * ez)
        sel.append((srow // size == scol // size) & (srow % size >= half) & (scol % size < half))
        size *= 2
    qe_all = q * jnp.exp2(b)
    kl_all = kk * jnp.exp2(bcast_rows(b, CHUNK, CHUNK - 1) - b)

    b_last = [blk(b, h, c)[CHUNK - 1:CHUNK] for h, c in probs]
    qe = [blk(qe_all, h, c) for h, c in probs]
    vc = [blk(v, h, c) for h, c in probs]

    s_lv = [[_dot_nt(blk(ql, h, c), blk(kl, h, c)) for h, c in probs] for ql, kl in zip(q_lv, k_lv)]
    a = []
    for j in range(len(probs)):
        acc = jnp.zeros((CHUNK, CHUNK), F32)
        for lv in range(len(sel)):
            acc = jnp.where(sel[lv], s_lv[lv][j], acc)
        a.append(acc)
    u = [_dot_tn(x, blk(kl_all, h, c)) for x, (h, c) in zip(vc, probs)]
    qk = q * kk
    o_intra = [_dot(x, y) + jnp.sum(blk(qk, h, c), axis=-1, keepdims=True) * y
               for x, y, (h, c) in zip(a, vc, probs)]

    st = list(st)
    outs = [[None] * len(chunks) for _ in heads]
    for c in chunks:
        for h in heads:
            j = h * len(chunks) + c
            outs[h][c] = o_intra[j] + _dot_nt(qe[j], st[h])
            st[h] = st[h] * jnp.exp2(b_last[j]) + u[j]

    cols = []
    for h in heads:
        o = jnp.concatenate(outs[h], axis=0)
        cols.append(o * lax.rsqrt(jnp.mean(o * o, axis=-1, keepdims=True) + NORM_EPS))
    o = jnp.concatenate(cols, axis=1)
    return _bf(o * norm_w * _silu(x[:, 3 * w:])), st


def _in_hgrn2_kernel(x_ref, nw_ref, w_ref, lbl_ref, hnw_ref, mu_ref, ohg_ref, rw_ref, st_ref, prev_ref):
    @pl.when(pl.program_id(1) == 0)
    def _():
        st_ref[...] = jnp.zeros_like(st_ref)
        prev_ref[...] = jnp.zeros_like(prev_ref)

    hb = _bf(_rmsnorm(x_ref[...], nw_ref[...]))
    rw = jnp.dot(hb, w_ref[:, HG_COLS:], preferred_element_type=F32)
    hg = jnp.dot(hb, w_ref[:, :HG_COLS], preferred_element_type=F32)
    rw_shift = _shift_rows(rw, prev_ref[...], 1)
    prev_ref[...] = rw[rw.shape[0] - 8:]
    rw_ref[...] = rw + (rw_shift - rw) * mu_ref[...]

    logits = lbl_ref[...]
    e = jnp.exp(logits - jnp.max(logits, axis=0, keepdims=True))
    lb = e[0:1] / jnp.sum(e, axis=0, keepdims=True)

    st = [st_ref[h] for h in range(HG_HEADS)]
    for i in range(ROW_TILE // SCAN_TILE):
        rows = slice(i * SCAN_TILE, (i + 1) * SCAN_TILE)
        ohg_ref[rows, :], st = _hgrn2_rows(hg[rows], lb, hnw_ref[...], st)
    for h in range(HG_HEADS):
        st_ref[h] = st[h]


def _in_hgrn2(x2, norm_w, w_in_bf, lb_logits, hg_norm_w, rw_mu, batch, seq):
    m = x2.shape[0]
    nt = seq // ROW_TILE

    def rows(width):
        return pl.BlockSpec((ROW_TILE, width), lambda b, t: (b * nt + t, 0))

    def whole(shape):
        return pl.BlockSpec(shape, lambda b, t: (0, 0), pipeline_mode=pl.Buffered(1))

    return pl.pallas_call(
        _in_hgrn2_kernel,
        grid=(batch, nt),
        in_specs=[rows(D_MODEL), whole((1, D_MODEL)), whole((D_MODEL, IN_COLS)),
                  whole((2, HG_WIDTH)), whole((1, HG_WIDTH)), whole((1, RW_COLS))],
        out_specs=[rows(HG_WIDTH), rows(RW_COLS)],
        out_shape=[
            jax.ShapeDtypeStruct((m, HG_WIDTH), BF16),
            jax.ShapeDtypeStruct((m, RW_COLS), F32),
        ],
        scratch_shapes=[pltpu.VMEM((HG_HEADS, HG_HEAD_DIM, HG_HEAD_DIM), F32),
                        pltpu.VMEM((8, RW_COLS), F32)],
        compiler_params=pltpu.CompilerParams(
            dimension_semantics=("arbitrary", "arbitrary"), vmem_limit_bytes=VMEM_LIMIT_BYTES),
        name="in_hgrn2",
    )(x2, norm_w, w_in_bf, lb_logits, hg_norm_w, rw_mu)


def _stack2(x):
    lane = _iota2(x.shape, 1)
    return jnp.concatenate([jnp.where(lane < RW_HEAD_DIM, x, 0.0),
                            jnp.where(lane >= RW_HEAD_DIM, x, 0.0)], axis=0)


def _rwkv7_kernel(x_ref, w0_ref, wwa_ref, a0_ref, g2_ref, kk_ref, ka_ref, rk_ref, lnw_ref, lnb_ref,
                  o_ref, z_ref):
    @pl.when(pl.program_id(1) == 0)
    def _():
        z_ref[...] = jnp.zeros_like(z_ref)

    n = x_ref.shape[0]
    w = RW_WIDTH
    r, k, v = x_ref[:, :w], x_ref[:, w:2 * w], x_ref[:, 2 * w:3 * w]
    wa = x_ref[:, 3 * w:3 * w + LANES]
    gd = x_ref[:, 3 * w + LANES:]

    wa = jnp.where(_iota2(wa.shape, 1) < RW_DECAY_LORA, jnp.tanh(wa), wa)
    lora = _dot(wa, wwa_ref[...])
    zw = -(w0_ref[...] + lora[:, :w])
    softplus = jnp.maximum(zw, 0.0) + jnp.log(1.0 + jnp.exp(-jnp.abs(zw)))
    logw = -jnp.exp(-softplus - 0.5)
    a = jax.nn.sigmoid(a0_ref[...] + lora[:, w:])
    g = _dot(jax.nn.sigmoid(gd), g2_ref[...])

    lane = _iota2((LANES, LANES), 1)
    row = _iota2((LANES, LANES), 0)
    same_head = (row // RW_HEAD_DIM) == (lane // RW_HEAD_DIM)
    head_ones = jnp.where(same_head, 1.0, 0.0).astype(F32)

    def head_sums(t):
        return jnp.concatenate(
            [_dot_exact_lhs(t[:, p * LANES:(p + 1) * LANES], head_ones) for p in range(RW_PAIRS)], axis=1)

    kk = k * kk_ref[...]
    kk = kk / jnp.maximum(jnp.sqrt(head_sums(kk * kk)), L2_EPS)
    k = k * (1.0 + (a - 1.0) * ka_ref[...])
    alpha = -kk
    beta = kk * a
    bonus = head_sums(r * k * rk_ref[...]) * v

    cum = _dot_exact_rhs(_chunk_tril(n), logw)
    e_pos = jnp.exp(cum)
    e_neg = jnp.exp(-cum)
    a_t = alpha * jnp.exp(cum - logw)
    r_t = r * e_pos
    b_t = beta * e_neg
    k_t = k * e_neg

    wrow = _iota2((CHUNK, LANES), 0)
    wcol = _iota2((CHUNK, LANES), 1) % CHUNK
    strict = wcol < wrow
    incl = wcol <= wrow
    eye_w = jnp.where(wcol == wrow, 1.0, 0.0).astype(F32)
    eye = lane == row

    pairs = range(RW_PAIRS)
    chunks = range(n // CHUNK)
    probs = [(p, c) for p in pairs for c in chunks]

    def blk(t, p, c):
        return t[c * CHUNK:(c + 1) * CHUNK, p * LANES:(p + 1) * LANES]

    c_last = [blk(cum, p, c)[CHUNK - 1:CHUNK] for p, c in probs]
    e_last = [jnp.exp(cl - blk(cum, p, c)) for (p, c), cl in zip(probs, c_last)]
    a_c = [blk(a_t, p, c) for p, c in probs]
    r_c = [blk(r_t, p, c) for p, c in probs]
    b_h = [blk(beta, p, c) * e for (p, c), e in zip(probs, e_last)]
    k_h = [blk(k, p, c) * e for (p, c), e in zip(probs, e_last)]
    vc = [blk(v, p, c) for p, c in probs]
    v_st = [_stack2(t) for t in vc]

    s4 = [_dot_nt(jnp.concatenate([aa, rr], axis=0),
                  jnp.concatenate([_stack2(blk(b_t, p, c)), _stack2(blk(k_t, p, c))], axis=0))
          for aa, rr, (p, c) in zip(a_c, r_c, probs)]
    l_ab = [jnp.where(strict, s[:CHUNK, :LANES], 0.0) for s in s4]
    l_ak = [jnp.where(strict, s[:CHUNK, LANES:], 0.0) for s in s4]
    a_rb = [jnp.where(incl, s[CHUNK:, :LANES], 0.0) for s in s4]
    a_rk = [jnp.where(incl, s[CHUNK:, LANES:], 0.0) for s in s4]

    tm = [l + eye_w for l in l_ab]
    lp = [_dot(l, _stack2(l)) for l in l_ab]
    w1 = [_dot(l, t) for l, t in zip(l_ak, v_st)]
    for _ in range(4):
        both = [_dot(jnp.concatenate([t, l], axis=0), _stack2(l)) for t, l in zip(tm, lp)]
        tm = [t + bo[:CHUNK] for t, bo in zip(tm, both)]
        lp = [bo[CHUNK:] for bo in both]
    tm = [t + _dot(t, _stack2(l)) for t, l in zip(tm, lp)]

    mp = [_dot(t, jnp.concatenate([_stack2(aa), _stack2(ww)], axis=1))
          for t, aa, ww in zip(tm, a_c, w1)]
    ry = [_dot(jnp.concatenate([lb, lk], axis=1),
               jnp.concatenate([jnp.concatenate([_stack2(t[:, :LANES]), _stack2(t[:, LANES:])], axis=1),
                                jnp.concatenate([jnp.zeros_like(vs), vs], axis=1)], axis=0))
          for lb, lk, t, vs in zip(a_rb, a_rk, mp, v_st)]
    rq = [rr + t[:, :LANES] for rr, t in zip(r_c, ry)]
    y0 = [t[:, LANES:] for t in ry]
    gu = [_dot_tn(jnp.concatenate([bh, kh], axis=0),
                  jnp.concatenate([t, jnp.concatenate([jnp.zeros_like(vv), vv], axis=1)], axis=0))
          for bh, kh, t, vv in zip(b_h, k_h, mp, vc)]
    g_mat = [jnp.where(same_head, t[:, :LANES], 0.0) + jnp.where(eye, jnp.exp(cl), 0.0)
             for t, cl in zip(gu, c_last)]
    u_mat = [jnp.where(same_head, t[:, LANES:], 0.0) for t in gu]

    z = [z_ref[p] for p in pairs]
    ys = [[None] * len(chunks) for _ in pairs]
    for c in chunks:
        for p in pairs:
            j = p * len(chunks) + c
            out = _dot(jnp.concatenate([g_mat[j], rq[j]], axis=0), z[p])
            ys[p][c] = out[LANES:] + y0[j]
            z[p] = out[:LANES] + u_mat[j]
    for p in pairs:
        z_ref[p] = z[p]

    y = jnp.concatenate([jnp.concatenate(ys[p], axis=0) for p in pairs], axis=1)
    inv_n = 1.0 / RW_HEAD_DIM
    mean = head_sums(y) * inv_n
    yc = y - mean
    var = head_sums(yc * yc) * inv_n
    y = yc * lax.rsqrt(var + RW_GN_EPS) * lnw_ref[...] + lnb_ref[...]
    o_ref[...] = _bf((y + bonus) * g)


def _rwkv7(rw_p, w0, wwa_bf, a0, g2_bf, k_k, k_a, r_k, ln_w, ln_b, batch, seq):
    m = rw_p.shape[0]
    nt = seq // SCAN_TILE

    def vec(width):
        return pl.BlockSpec((1, width), lambda b, t: (0, 0))

    return pl.pallas_call(
        _rwkv7_kernel,
        grid=(batch, nt),
        in_specs=[
            pl.BlockSpec((SCAN_TILE, RW_COLS), lambda b, t: (b * nt + t, 0)),
            vec(RW_WIDTH),
            pl.BlockSpec((LANES, 2 * RW_WIDTH), lambda b, t: (0, 0)),
            vec(RW_WIDTH),
            pl.BlockSpec((RW_GATE_LORA, RW_WIDTH), lambda b, t: (0, 0)),
            vec(RW_WIDTH), vec(RW_WIDTH), vec(RW_WIDTH), vec(RW_WIDTH), vec(RW_WIDTH),
        ],
        out_specs=pl.BlockSpec((SCAN_TILE, RW_WIDTH), lambda b, t: (b * nt + t, 0)),
        out_shape=jax.ShapeDtypeStruct((m, RW_WIDTH), BF16),
        scratch_shapes=[pltpu.VMEM((RW_PAIRS, LANES, LANES), F32)],
        compiler_params=pltpu.CompilerParams(
            dimension_semantics=("arbitrary", "arbitrary"), vmem_limit_bytes=VMEM_LIMIT_BYTES),
        name="rwkv7",
    )(rw_p, w0, wwa_bf, a0, g2_bf, k_k, k_a, r_k, ln_w, ln_b)


def _mix_ffn_kernel(x_ref, ohg_ref, orw_ref, wo_ref, n2_ref, wu_ref, cw_ref, cb_ref, wd_ref, nf_ref,
                    o_ref, prev_ref, u0_ref, u1_ref, act_ref):
    u_refs = (u0_ref, u1_ref)
    @pl.when(pl.program_id(1) == 0)
    def _():
        prev_ref[...] = jnp.zeros_like(prev_ref)

    x1 = (x_ref[...]
          + jnp.dot(ohg_ref[...], wo_ref[:HG_WIDTH], preferred_element_type=F32)
          + jnp.dot(orw_ref[...], wo_ref[HG_WIDTH:], preferred_element_type=F32))
    h = _bf(_rmsnorm(x1, n2_ref[...]))
    n = h.shape[0]

    ct = FFN_COL_TILE
    ntile = D_FF // ct

    def halves(j):
        return ((slice(0, ct), slice(j * ct, (j + 1) * ct)),
                (slice(ct, 2 * ct), slice(D_FF + j * ct, D_FF + (j + 1) * ct)))

    def up(j):
        u = u_refs[j % 2]
        for dst, cols in halves(j):
            u[0:8, dst] = prev_ref[:, cols]
            u[8:, dst] = jnp.dot(h, wu_ref[:, cols], preferred_element_type=F32)
            prev_ref[:, cols] = u[n:n + 8, dst]

    def conv_gate(j):
        u = u_refs[j % 2]
        c = []
        for dst, cols in halves(j):
            cw = cw_ref[:, cols]
            c.append(cw[2:3] * u[8:n + 8, dst] + cw[1:2] * u[7:n + 7, dst] + cw[0:1] * u[6:n + 6, dst]
                     + cb_ref[:, cols])
        act_ref[:, j * ct:(j + 1) * ct] = _bf(_silu(c[0]) * c[1])

    up(0)
    for j in range(ntile):
        if j + 1 < ntile:
            up(j + 1)
        conv_gate(j)
    y = x1 + jnp.dot(act_ref[...], wd_ref[...], preferred_element_type=F32)
    o_ref[...] = _rmsnorm(y, nf_ref[...])


def _mix_ffn(x2, o_hg, o_rw, w_out_bf, norm2_w, w_up_bf, conv_w, conv_b, w_down_bf, final_w, batch, seq):
    m = x2.shape[0]
    nt = seq // ROW_TILE

    def rows(width):
        return pl.BlockSpec((ROW_TILE, width), lambda b, t: (b * nt + t, 0))

    def whole(shape):
        return pl.BlockSpec(shape, lambda b, t: (0, 0), pipeline_mode=pl.Buffered(1))

    return pl.pallas_call(
        _mix_ffn_kernel,
        grid=(batch, nt),
        in_specs=[
            rows(D_MODEL), rows(HG_WIDTH), rows(RW_WIDTH),
            whole((D_MODEL, D_MODEL)), whole((1, D_MODEL)),
            whole((D_MODEL, 2 * D_FF)), whole((3, 2 * D_FF)), whole((1, 2 * D_FF)),
            whole((D_FF, D_MODEL)), whole((1, D_MODEL)),
        ],
        out_specs=rows(D_MODEL),
        out_shape=jax.ShapeDtypeStruct((m, D_MODEL), F32),
        scratch_shapes=[pltpu.VMEM((8, 2 * D_FF), F32),
                        pltpu.VMEM((8 + ROW_TILE, 2 * FFN_COL_TILE), F32),
                        pltpu.VMEM((8 + ROW_TILE, 2 * FFN_COL_TILE), F32),
                        pltpu.VMEM((ROW_TILE, D_FF), BF16)],
        compiler_params=pltpu.CompilerParams(
            dimension_semantics=("arbitrary", "arbitrary"), vmem_limit_bytes=VMEM_LIMIT_BYTES),
        name="mix_ffn",
    )(x2, o_hg, o_rw, w_out_bf, norm2_w, w_up_bf, conv_w, conv_b, w_down_bf, final_w)


def kernel(x, norm1_w, w_in, hg_lb_logits, hg_norm_w, rw_shift_mu, rw_w0, rw_w2, rw_a0, rw_a2, rw_g2,
           rw_k_k, rw_k_a, rw_r_k, rw_ln_w, rw_ln_b, w_out, norm2_w, w_up, conv_w, conv_b, w_down,
           final_norm_w):
    batch, seq, _ = x.shape
    assert seq % SCAN_TILE == 0 and seq % ROW_TILE == 0 and norm1_w.shape[0] == 1
    assert CHUNK == RW_HEAD_DIM and 2 * RW_HEAD_DIM == LANES
    x2 = x.reshape(batch * seq, D_MODEL)

    zeros = jnp.zeros_like(rw_w2[0])
    wwa = _bf(jnp.concatenate([jnp.concatenate([rw_w2[0], zeros], axis=1),
                               jnp.concatenate([zeros, rw_a2[0]], axis=1)], axis=0))

    o_hg, rw_p = _in_hgrn2(x2, norm1_w, _bf(w_in[0]), hg_lb_logits, hg_norm_w, rw_shift_mu, batch, seq)
    o_rw = _rwkv7(rw_p, rw_w0, wwa, rw_a0, _bf(rw_g2[0]), rw_k_k, rw_k_a, rw_r_k, rw_ln_w, rw_ln_b, batch, seq)
    out = _mix_ffn(x2, o_hg, o_rw, _bf(w_out[0]), norm2_w, _bf(w_up[0]), conv_w[0], conv_b, _bf(w_down[0]),
                   final_norm_w.reshape(1, D_MODEL), batch, seq)
    return out.reshape(batch, seq, D_MODEL)
```

```python
import jax
import jax.numpy as jnp
from jax import lax
from jax.experimental import pallas as pl
from jax.experimental.pallas import tpu as pltpu

F32 = jnp.float32
BF16 = jnp.bfloat16

D_MODEL = 1024
HG_WIDTH = 512
HG_HEAD_DIM = 128
HG_HEADS = HG_WIDTH // HG_HEAD_DIM
RW_WIDTH = 512
RW_HEAD_DIM = 64
RW_DECAY_LORA = 64
RW_AAA_LORA = 64
RW_GATE_LORA = 128
RW_COLS = 3 * RW_WIDTH + RW_DECAY_LORA + RW_AAA_LORA + RW_GATE_LORA
HG_COLS = 4 * HG_WIDTH
IN_COLS = HG_COLS + RW_COLS
D_FF = 2816
NORM_EPS = 1e-6
RW_GN_EPS = 64e-5
L2_EPS = 1e-12

LANES = 128
CHUNK = 64
RW_PAIRS = RW_WIDTH // LANES
VMEM_LIMIT_BYTES = 56 * 1024 * 1024

ROW_TILE = 512
SCAN_TILE = 512
RW_BLOCK = 256
HG_TILE = 256
FFN_COL_TILE = 256


def _bf(x):
    return x.astype(BF16)


def _dot(a, b):
    return jnp.dot(_bf(a), _bf(b), preferred_element_type=F32)


def _dot_nt(a, b):
    return lax.dot_general(_bf(a), _bf(b), (((1,), (1,)), ((), ())), preferred_element_type=F32)


def _dot_tn(a, b):
    return jnp.dot(_bf(a.T), _bf(b), preferred_element_type=F32)


def _split3(x):
    h1 = _bf(x)
    r1 = x - h1.astype(F32)
    h2 = _bf(r1)
    r2 = r1 - h2.astype(F32)
    return h1, h2, _bf(r2)


def _dot_exact_rhs(m01, x):
    w = x.shape[1]
    out = jnp.dot(_bf(m01), jnp.concatenate(_split3(x), axis=1), preferred_element_type=F32)
    return out[:, :w] + out[:, w:2 * w] + out[:, 2 * w:]


def _dot_exact_lhs(x, m01):
    n = x.shape[0]
    out = jnp.dot(jnp.concatenate(_split3(x), axis=0), _bf(m01), preferred_element_type=F32)
    return out[:n] + out[n:2 * n] + out[2 * n:]


def _iota2(shape, dim):
    return lax.broadcasted_iota(jnp.int32, shape, dim)


def _chunk_tril(n):
    r = _iota2((n, n), 0)
    c = _iota2((n, n), 1)
    return jnp.where((r // CHUNK == c // CHUNK) & (c <= r), 1.0, 0.0).astype(F32)


def _shift_rows(x, prev_rows, k):
    rolled = pltpu.roll(x, k, axis=0)
    prev = pltpu.roll(prev_rows, k, axis=0)
    head = jnp.where(_iota2(prev.shape, 0) < k, prev, rolled[:8])
    return jnp.concatenate([head, rolled[8:]], axis=0)


def _silu(x):
    return x * jax.nn.sigmoid(x)


def _rmsnorm(x, w):
    return x * lax.rsqrt(jnp.mean(x * x, axis=-1, keepdims=True) + NORM_EPS) * w


def _hgrn2_rows(x, lb, norm_w, st):
    w = HG_WIDTH
    f = lb + (1.0 - lb) * jax.nn.sigmoid(x[:, w:2 * w])
    kk = 1.0 - f
    q = _silu(x[:, :w]) * (HG_HEAD_DIM ** -0.5)
    v = x[:, 2 * w:3 * w]
    n = q.shape[0]
    b = _dot_exact_rhs(_chunk_tril(n), jnp.log2(f))

    heads = range(HG_HEADS)
    chunks = range(n // CHUNK)
    probs = [(h, c) for h in heads for c in chunks]

    def blk(x, h, c):
        return x[c * CHUNK:(c + 1) * CHUNK, h * LANES:(h + 1) * LANES]

    def bcast_rows(x, period, offset):
        return jnp.concatenate(
            [jnp.broadcast_to(x[g * period + offset:g * period + offset + 1], (period, x.shape[1]))
             for g in range(x.shape[0] // period)], axis=0)

    t_idx = _iota2(b.shape, 0)
    srow = _iota2((CHUNK, CHUNK), 0)
    scol = _iota2((CHUNK, CHUNK), 1)
    q_bf, k_bf = _bf(q), _bf(kk)
    q_lv, k_lv, sel = [_bf(q * f)], [k_bf], []
    size = 2
    while size <= CHUNK:
        half = size // 2
        if size == 4:
            pos = t_idx % size
            m = jnp.where(pos == 0, pltpu.roll(b, n - 1, axis=0),
                          jnp.where(pos == 1, b,
                                    jnp.where(pos == 2, pltpu.roll(b, 1, axis=0), pltpu.roll(b, 2, axis=0))))
        elif size > 4:
            m = bcast_rows(b, size, half - 1)
        if size > 2:
            ez = _bf(jnp.exp2(-jnp.abs(b - m)))
            q_lv.append(q_bf * ez)
            k_lv.append(k_bf * ez)
        sel.append((srow // size == scol // size) & (srow % size >= half) & (scol % size < half))
        size *= 2
    qe_all = q * jnp.exp2(b)
    kl_all = kk * jnp.exp2(bcast_rows(b, CHUNK, CHUNK - 1) - b)

    b_last = [blk(b, h, c)[CHUNK - 1:CHUNK] for h, c in probs]
    qe = [blk(qe_all, h, c) for h, c in probs]
    vc = [blk(v, h, c) for h, c in probs]

    s_lv = [[_dot_nt(blk(ql, h, c), blk(kl, h, c)) for h, c in probs] for ql, kl in zip(q_lv, k_lv)]
    a = []
    for j in range(len(probs)):
        acc = jnp.zeros((CHUNK, CHUNK), F32)
        for lv in range(len(sel)):
            acc = jnp.where(sel[lv], s_lv[lv][j], acc)
        a.append(acc)
    u = [_dot_tn(x, blk(kl_all, h, c)) for x, (h, c) in zip(vc, probs)]
    qk = q * kk
    o_intra = [_dot(x, y) + jnp.sum(blk(qk, h, c), axis=-1, keepdims=True) * y
               for x, y, (h, c) in zip(a, vc, probs)]

    st = list(st)
    outs = [[None] * len(chunks) for _ in heads]
    for c in chunks:
        for h in heads:
            j = h * len(chunks) + c
            outs[h][c] = o_intra[j] + _dot_nt(qe[j], st[h])
            st[h] = st[h] * jnp.exp2(b_last[j]) + u[j]

    cols = []
    for h in heads:
        o = jnp.concatenate(outs[h], axis=0)
        cols.append(o * lax.rsqrt(jnp.mean(o * o, axis=-1, keepdims=True) + NORM_EPS))
    o = jnp.concatenate(cols, axis=1)
    return _bf(o * norm_w * _silu(x[:, 3 * w:])), st


def _in_hgrn2_kernel(x_ref, nw_ref, w_ref, lbl_ref, hnw_ref, mu_ref, ohg_ref, rw_ref, st_ref, prev_ref):
    @pl.when(pl.program_id(1) == 0)
    def _():
        st_ref[...] = jnp.zeros_like(st_ref)
        prev_ref[...] = jnp.zeros_like(prev_ref)

    hb = _bf(_rmsnorm(x_ref[...], nw_ref[...]))
    rw = jnp.dot(hb, w_ref[:, HG_COLS:], preferred_element_type=F32)
    hg = jnp.dot(hb, w_ref[:, :HG_COLS], preferred_element_type=F32)
    rw_shift = _shift_rows(rw, prev_ref[...], 1)
    prev_ref[...] = rw[rw.shape[0] - 8:]
    rw_ref[...] = rw + (rw_shift - rw) * mu_ref[...]

    logits = lbl_ref[...]
    e = jnp.exp(logits - jnp.max(logits, axis=0, keepdims=True))
    lb = e[0:1] / jnp.sum(e, axis=0, keepdims=True)

    st = [st_ref[h] for h in range(HG_HEADS)]
    for i in range(ROW_TILE // HG_TILE):
        rows = slice(i * HG_TILE, (i + 1) * HG_TILE)
        ohg_ref[rows, :], st = _hgrn2_rows(hg[rows], lb, hnw_ref[...], st)
    for h in range(HG_HEADS):
        st_ref[h] = st[h]


def _in_hgrn2(x2, norm_w, w_in_bf, lb_logits, hg_norm_w, rw_mu, batch, seq):
    m = x2.shape[0]
    nt = seq // ROW_TILE

    def rows(width):
        return pl.BlockSpec((ROW_TILE, width), lambda b, t: (b * nt + t, 0))

    def whole(shape):
        return pl.BlockSpec(shape, lambda b, t: (0, 0), pipeline_mode=pl.Buffered(1))

    return pl.pallas_call(
        _in_hgrn2_kernel,
        grid=(batch, nt),
        in_specs=[rows(D_MODEL), whole((1, D_MODEL)), whole((D_MODEL, IN_COLS)),
                  whole((2, HG_WIDTH)), whole((1, HG_WIDTH)), whole((1, RW_COLS))],
        out_specs=[rows(HG_WIDTH), rows(RW_COLS)],
        out_shape=[
            jax.ShapeDtypeStruct((m, HG_WIDTH), BF16),
            jax.ShapeDtypeStruct((m, RW_COLS), F32),
        ],
        scratch_shapes=[pltpu.VMEM((HG_HEADS, HG_HEAD_DIM, HG_HEAD_DIM), F32),
                        pltpu.VMEM((8, RW_COLS), F32)],
        compiler_params=pltpu.CompilerParams(
            dimension_semantics=("arbitrary", "arbitrary"), vmem_limit_bytes=VMEM_LIMIT_BYTES),
        name="in_hgrn2",
    )(x2, norm_w, w_in_bf, lb_logits, hg_norm_w, rw_mu)


def _stack2(x):
    lane = _iota2(x.shape, 1)
    return jnp.concatenate([jnp.where(lane < RW_HEAD_DIM, x, 0.0),
                            jnp.where(lane >= RW_HEAD_DIM, x, 0.0)], axis=0)


def _rwkv7_kernel(x_ref, w0_ref, wwa_ref, a0_ref, g2_ref, kk_ref, ka_ref, rk_ref, lnw_ref, lnb_ref,
                  o_ref, z_ref):
    @pl.when(pl.program_id(1) == 0)
    def _():
        z_ref[...] = jnp.zeros_like(z_ref)

    w = RW_WIDTH
    lane = _iota2((LANES, LANES), 1)
    row = _iota2((LANES, LANES), 0)
    same_head = (row // RW_HEAD_DIM) == (lane // RW_HEAD_DIM)
    head_ones = jnp.where(same_head, 1.0, 0.0).astype(F32)

    def head_sums(t):
        return jnp.concatenate(
            [_dot_exact_lhs(t[:, p * LANES:(p + 1) * LANES], head_ones) for p in range(RW_PAIRS)], axis=1)

    wrow = _iota2((CHUNK, LANES), 0)
    wcol = _iota2((CHUNK, LANES), 1) % CHUNK
    strict = wcol < wrow
    incl = wcol <= wrow
    eye_w = jnp.where(wcol == wrow, 1.0, 0.0).astype(F32)
    eye = lane == row
    pairs = range(RW_PAIRS)

    def gates(rows):
        r, k, v = x_ref[rows, :w], x_ref[rows, w:2 * w], x_ref[rows, 2 * w:3 * w]
        wa = x_ref[rows, 3 * w:3 * w + LANES]
        gd = x_ref[rows, 3 * w + LANES:]
        n = r.shape[0]

        wa = jnp.where(_iota2(wa.shape, 1) < RW_DECAY_LORA, jnp.tanh(wa), wa)
        lora = _dot(wa, wwa_ref[...])
        zw = -(w0_ref[...] + lora[:, :w])
        softplus = jnp.maximum(zw, 0.0) + jnp.log(1.0 + jnp.exp(-jnp.abs(zw)))
        logw = -jnp.exp(-softplus - 0.5)
        a = jax.nn.sigmoid(a0_ref[...] + lora[:, w:])
        g = _dot(jax.nn.sigmoid(gd), g2_ref[...])

        kk = k * kk_ref[...]
        kk = kk / jnp.maximum(jnp.sqrt(head_sums(kk * kk)), L2_EPS)
        k = k * (1.0 + (a - 1.0) * ka_ref[...])
        beta = kk * a
        bonus = head_sums(r * k * rk_ref[...]) * v

        cum = _dot_exact_rhs(_chunk_tril(n), logw)
        e_neg = jnp.exp(-cum)
        return dict(n=n, cum=cum, a_t=-kk * jnp.exp(cum - logw), r_t=r * jnp.exp(cum), b_t=beta * e_neg,
                    k_t=k * e_neg, beta=beta, k=k, v=v, bonus=bonus, g=g)

    def recur(d, z):
        chunks = range(d["n"] // CHUNK)
        probs = [(p, c) for p in pairs for c in chunks]

        def blk(t, p, c):
            return t[c * CHUNK:(c + 1) * CHUNK, p * LANES:(p + 1) * LANES]

        c_last = [blk(d["cum"], p, c)[CHUNK - 1:CHUNK] for p, c in probs]
        e_last = [jnp.exp(cl - blk(d["cum"], p, c)) for (p, c), cl in zip(probs, c_last)]
        a_c = [blk(d["a_t"], p, c) for p, c in probs]
        r_c = [blk(d["r_t"], p, c) for p, c in probs]
        b_h = [blk(d["beta"], p, c) * e for (p, c), e in zip(probs, e_last)]
        k_h = [blk(d["k"], p, c) * e for (p, c), e in zip(probs, e_last)]
        vc = [blk(d["v"], p, c) for p, c in probs]
        v_st = [_stack2(t) for t in vc]

        s4 = [_dot_nt(jnp.concatenate([aa, rr], axis=0),
                      jnp.concatenate([_stack2(blk(d["b_t"], p, c)), _stack2(blk(d["k_t"], p, c))], axis=0))
              for aa, rr, (p, c) in zip(a_c, r_c, probs)]
        l_ab = [jnp.where(strict, s[:CHUNK, :LANES], 0.0) for s in s4]
        l_ak = [jnp.where(strict, s[:CHUNK, LANES:], 0.0) for s in s4]
        a_rb = [jnp.where(incl, s[CHUNK:, :LANES], 0.0) for s in s4]
        a_rk = [jnp.where(incl, s[CHUNK:, LANES:], 0.0) for s in s4]

        tm = [l + eye_w for l in l_ab]
        lp = [_dot(l, _stack2(l)) for l in l_ab]
        w1 = [_dot(l, t) for l, t in zip(l_ak, v_st)]
        for _ in range(4):
            both = [_dot(jnp.concatenate([t, l], axis=0), _stack2(l)) for t, l in zip(tm, lp)]
            tm = [t + bo[:CHUNK] for t, bo in zip(tm, both)]
            lp = [bo[CHUNK:] for bo in both]
        tm = [t + _dot(t, _stack2(l)) for t, l in zip(tm, lp)]

        mp = [_dot(t, jnp.concatenate([_stack2(aa), _stack2(ww)], axis=1))
              for t, aa, ww in zip(tm, a_c, w1)]
        ry = [_dot(jnp.concatenate([lb, lk], axis=1),
                   jnp.concatenate([jnp.concatenate([_stack2(t[:, :LANES]), _stack2(t[:, LANES:])], axis=1),
                                    jnp.concatenate([jnp.zeros_like(vs), vs], axis=1)], axis=0))
              for lb, lk, t, vs in zip(a_rb, a_rk, mp, v_st)]
        rq = [rr + t[:, :LANES] for rr, t in zip(r_c, ry)]
        y0 = [t[:, LANES:] for t in ry]
        gu = [_dot_tn(jnp.concatenate([bh, kh], axis=0),
                      jnp.concatenate([t, jnp.concatenate([jnp.zeros_like(vv), vv], axis=1)], axis=0))
              for bh, kh, t, vv in zip(b_h, k_h, mp, vc)]
        g_mat = [jnp.where(same_head, t[:, :LANES], 0.0) + jnp.where(eye, jnp.exp(cl), 0.0)
                 for t, cl in zip(gu, c_last)]
        u_mat = [jnp.where(same_head, t[:, LANES:], 0.0) for t in gu]

        z = list(z)
        ys = [[None] * len(chunks) for _ in pairs]
        for c in chunks:
            for p in pairs:
                j = p * len(chunks) + c
                out = _dot(jnp.concatenate([g_mat[j], rq[j]], axis=0), z[p])
                ys[p][c] = out[LANES:] + y0[j]
                z[p] = out[:LANES] + u_mat[j]
        return jnp.concatenate([jnp.concatenate(ys[p], axis=0) for p in pairs], axis=1), z

    def finish(y, d):
        inv_n = 1.0 / RW_HEAD_DIM
        mean = head_sums(y) * inv_n
        yc = y - mean
        var = head_sums(yc * yc) * inv_n
        y = yc * lax.rsqrt(var + RW_GN_EPS) * lnw_ref[...] + lnb_ref[...]
        return _bf((y + d["bonus"]) * d["g"])

    z = [z_ref[p] for p in pairs]
    for i in range(x_ref.shape[0] // RW_BLOCK):
        rows = slice(i * RW_BLOCK, (i + 1) * RW_BLOCK)
        d = gates(rows)
        y, z = recur(d, z)
        o_ref[rows, :] = finish(y, d)
    for p in pairs:
        z_ref[p] = z[p]


def _rwkv7(rw_p, w0, wwa_bf, a0, g2_bf, k_k, k_a, r_k, ln_w, ln_b, batch, seq):
    m = rw_p.shape[0]
    nt = seq // SCAN_TILE

    def vec(width):
        return pl.BlockSpec((1, width), lambda b, t: (0, 0))

    return pl.pallas_call(
        _rwkv7_kernel,
        grid=(batch, nt),
        in_specs=[
            pl.BlockSpec((SCAN_TILE, RW_COLS), lambda b, t: (b * nt + t, 0)),
            vec(RW_WIDTH),
            pl.BlockSpec((LANES, 2 * RW_WIDTH), lambda b, t: (0, 0)),
            vec(RW_WIDTH),
            pl.BlockSpec((RW_GATE_LORA, RW_WIDTH), lambda b, t: (0, 0)),
            vec(RW_WIDTH), vec(RW_WIDTH), vec(RW_WIDTH), vec(RW_WIDTH), vec(RW_WIDTH),
        ],
        out_specs=pl.BlockSpec((SCAN_TILE, RW_WIDTH), lambda b, t: (b * nt + t, 0)),
        out_shape=jax.ShapeDtypeStruct((m, RW_WIDTH), BF16),
        scratch_shapes=[pltpu.VMEM((RW_PAIRS, LANES, LANES), F32)],
        compiler_params=pltpu.CompilerParams(
            dimension_semantics=("arbitrary", "arbitrary"), vmem_limit_bytes=VMEM_LIMIT_BYTES),
        name="rwkv7",
    )(rw_p, w0, wwa_bf, a0, g2_bf, k_k, k_a, r_k, ln_w, ln_b)


def _mix_ffn_kernel(x_ref, ohg_ref, orw_ref, wo_ref, n2_ref, wu_ref, cw_ref, cb_ref, wd_ref, nf_ref,
                    o_ref, prev_ref, u0_ref, u1_ref, act_ref):
    u_refs = (u0_ref, u1_ref)
    @pl.when(pl.program_id(1) == 0)
    def _():
        prev_ref[...] = jnp.zeros_like(prev_ref)

    x1 = (x_ref[...]
          + jnp.dot(ohg_ref[...], wo_ref[:HG_WIDTH], preferred_element_type=F32)
          + jnp.dot(orw_ref[...], wo_ref[HG_WIDTH:], preferred_element_type=F32))
    h = _bf(_rmsnorm(x1, n2_ref[...]))
    n = h.shape[0]

    ct = FFN_COL_TILE
    ntile = D_FF // ct

    def halves(j):
        return ((slice(0, ct), slice(j * ct, (j + 1) * ct)),
                (slice(ct, 2 * ct), slice(D_FF + j * ct, D_FF + (j + 1) * ct)))

    def up(j):
        u = u_refs[j % 2]
        for dst, cols in halves(j):
            u[0:8, dst] = prev_ref[:, cols]
            u[8:, dst] = jnp.dot(h, wu_ref[:, cols], preferred_element_type=F32)
            prev_ref[:, cols] = u[n:n + 8, dst]

    def conv_gate(j):
        u = u_refs[j % 2]
        c = []
        for dst, cols in halves(j):
            cw = cw_ref[:, cols]
            c.append(cw[2:3] * u[8:n + 8, dst] + cw[1:2] * u[7:n + 7, dst] + cw[0:1] * u[6:n + 6, dst]
                     + cb_ref[:, cols])
        act_ref[:, j * ct:(j + 1) * ct] = _bf(_silu(c[0]) * c[1])

    up(0)
    for j in range(ntile):
        if j + 1 < ntile:
            up(j + 1)
        conv_gate(j)
    y = x1 + jnp.dot(act_ref[...], wd_ref[...], preferred_element_type=F32)
    o_ref[...] = _rmsnorm(y, nf_ref[...])


def _mix_ffn(x2, o_hg, o_rw, w_out_bf, norm2_w, w_up_bf, conv_w, conv_b, w_down_bf, final_w, batch, seq):
    m = x2.shape[0]
    nt = seq // ROW_TILE

    def rows(width):
        return pl.BlockSpec((ROW_TILE, width), lambda b, t: (b * nt + t, 0))

    def whole(shape):
        return pl.BlockSpec(shape, lambda b, t: (0, 0), pipeline_mode=pl.Buffered(1))

    return pl.pallas_call(
        _mix_ffn_kernel,
        grid=(batch, nt),
        in_specs=[
            rows(D_MODEL), rows(HG_WIDTH), rows(RW_WIDTH),
            whole((D_MODEL, D_MODEL)), whole((1, D_MODEL)),
            whole((D_MODEL, 2 * D_FF)), whole((3, 2 * D_FF)), whole((1, 2 * D_FF)),
            whole((D_FF, D_MODEL)), whole((1, D_MODEL)),
        ],
        out_specs=rows(D_MODEL),
        out_shape=jax.ShapeDtypeStruct((m, D_MODEL), F32),
        scratch_shapes=[pltpu.VMEM((8, 2 * D_FF), F32),
                        pltpu.VMEM((8 + ROW_TILE, 2 * FFN_COL_TILE), F32),
                        pltpu.VMEM((8 + ROW_TILE, 2 * FFN_COL_TILE), F32),
                        pltpu.VMEM((ROW_TILE, D_FF), BF16)],
        compiler_params=pltpu.CompilerParams(
            dimension_semantics=("arbitrary", "arbitrary"), vmem_limit_bytes=VMEM_LIMIT_BYTES),
        name="mix_ffn",
    )(x2, o_hg, o_rw, w_out_bf, norm2_w, w_up_bf, conv_w, conv_b, w_down_bf, final_w)


def kernel(x, norm1_w, w_in, hg_lb_logits, hg_norm_w, rw_shift_mu, rw_w0, rw_w2, rw_a0, rw_a2, rw_g2,
           rw_k_k, rw_k_a, rw_r_k, rw_ln_w, rw_ln_b, w_out, norm2_w, w_up, conv_w, conv_b, w_down,
           final_norm_w):
    batch, seq, _ = x.shape
    assert seq % SCAN_TILE == 0 and seq % ROW_TILE == 0 and norm1_w.shape[0] == 1
    assert CHUNK == RW_HEAD_DIM and 2 * RW_HEAD_DIM == LANES
    x2 = x.reshape(batch * seq, D_MODEL)

    zeros = jnp.zeros_like(rw_w2[0])
    wwa = _bf(jnp.concatenate([jnp.concatenate([rw_w2[0], zeros], axis=1),
                               jnp.concatenate([zeros, rw_a2[0]], axis=1)], axis=0))

    o_hg, rw_p = _in_hgrn2(x2, norm1_w, _bf(w_in[0]), hg_lb_logits, hg_norm_w, rw_shift_mu, batch, seq)
    o_rw = _rwkv7(rw_p, rw_w0, wwa, rw_a0, _bf(rw_g2[0]), rw_k_k, rw_k_a, rw_r_k, rw_ln_w, rw_ln_b, batch, seq)
    out = _mix_ffn(x2, o_hg, o_rw, _bf(w_out[0]), norm2_w, _bf(w_up[0]), conv_w[0], conv_b, _bf(w_down[0]),
                   final_norm_w.reshape(1, D_MODEL), batch, seq)
    return out.reshape(batch, seq, D_MODEL)
```

```python
import jax
import jax.numpy as jnp
from jax import lax
from jax.experimental import pallas as pl
from jax.experimental.pallas import tpu as pltpu

F32 = jnp.float32
BF16 = jnp.bfloat16

D_MODEL = 1024
HG_WIDTH = 512
HG_HEAD_DIM = 128
HG_HEADS = HG_WIDTH // HG_HEAD_DIM
RW_WIDTH = 512
RW_HEAD_DIM = 64
RW_DECAY_LORA = 64
RW_AAA_LORA = 64
RW_GATE_LORA = 128
RW_COLS = 3 * RW_WIDTH + RW_DECAY_LORA + RW_AAA_LORA + RW_GATE_LORA
HG_COLS = 4 * HG_WIDTH
IN_COLS = HG_COLS + RW_COLS
D_FF = 2816
NORM_EPS = 1e-6
RW_GN_EPS = 64e-5
L2_EPS = 1e-12

LANES = 128
SUBLANES = 8
CHUNK = 64
RW_PAIRS = RW_WIDTH // LANES
VMEM_LIMIT_BYTES = 56 * 1024 * 1024

ROW_TILE = 512
SCAN_TILE = 512
RW_BLOCK = 256
HG_TILE = 256
FFN_COL_TILE = 256


def _bf(x):
    return x.astype(BF16)


def _dot(a, b):
    return jnp.dot(_bf(a), _bf(b), preferred_element_type=F32)


def _dot_nt(a, b):
    return lax.dot_general(_bf(a), _bf(b), (((1,), (1,)), ((), ())), preferred_element_type=F32)


def _dot_tn(a, b):
    return jnp.dot(_bf(a.T), _bf(b), preferred_element_type=F32)


def _split3(x):
    h1 = _bf(x)
    r1 = x - h1.astype(F32)
    h2 = _bf(r1)
    r2 = r1 - h2.astype(F32)
    return h1, h2, _bf(r2)


def _dot_exact_rhs(m01, x):
    w = x.shape[1]
    out = jnp.dot(_bf(m01), jnp.concatenate(_split3(x), axis=1), preferred_element_type=F32)
    return out[:, :w] + out[:, w:2 * w] + out[:, 2 * w:]


def _dot_exact_lhs(x, m01):
    n = x.shape[0]
    out = jnp.dot(jnp.concatenate(_split3(x), axis=0), _bf(m01), preferred_element_type=F32)
    return out[:n] + out[n:2 * n] + out[2 * n:]


def _iota2(shape, dim):
    return lax.broadcasted_iota(jnp.int32, shape, dim)


def _chunk_tril(n):
    r = _iota2((n, n), 0)
    c = _iota2((n, n), 1)
    return jnp.where((r // CHUNK == c // CHUNK) & (c <= r), 1.0, 0.0).astype(F32)


def _shift_rows(x, prev_rows, k):
    rolled = pltpu.roll(x, k, axis=0)
    prev = pltpu.roll(prev_rows, k, axis=0)
    head = jnp.where(_iota2(prev.shape, 0) < k, prev, rolled[:SUBLANES])
    return jnp.concatenate([head, rolled[SUBLANES:]], axis=0)


def _silu(x):
    return x * jax.nn.sigmoid(x)


def _rmsnorm(x, w):
    return x * lax.rsqrt(jnp.mean(x * x, axis=-1, keepdims=True) + NORM_EPS) * w


def _hgrn2_rows(x, lb, norm_w, st):
    w = HG_WIDTH
    f = lb + (1.0 - lb) * jax.nn.sigmoid(x[:, w:2 * w])
    kk = 1.0 - f
    q = _silu(x[:, :w]) * (HG_HEAD_DIM ** -0.5)
    v = x[:, 2 * w:3 * w]
    n = q.shape[0]
    b = _dot_exact_rhs(_chunk_tril(n), jnp.log2(f))

    heads = range(HG_HEADS)
    chunks = range(n // CHUNK)
    probs = [(h, c) for h in heads for c in chunks]

    def blk(x, h, c):
        return x[c * CHUNK:(c + 1) * CHUNK, h * LANES:(h + 1) * LANES]

    def bcast_rows(x, period, offset):
        return jnp.concatenate(
            [jnp.broadcast_to(x[g * period + offset:g * period + offset + 1], (period, x.shape[1]))
             for g in range(x.shape[0] // period)], axis=0)

    t_idx = _iota2(b.shape, 0)
    srow = _iota2((CHUNK, CHUNK), 0)
    scol = _iota2((CHUNK, CHUNK), 1)
    q_bf, k_bf = _bf(q), _bf(kk)
    q_lv, k_lv, sel = [_bf(q * f)], [k_bf], []
    size = 2
    while size <= CHUNK:
        half = size // 2
        if size == 4:
            pos = t_idx % size
            m = jnp.where(pos == 0, pltpu.roll(b, n - 1, axis=0),
                          jnp.where(pos == 1, b,
                                    jnp.where(pos == 2, pltpu.roll(b, 1, axis=0), pltpu.roll(b, 2, axis=0))))
        elif size > 4:
            m = bcast_rows(b, size, half - 1)
        if size > 2:
            ez = _bf(jnp.exp2(-jnp.abs(b - m)))
            q_lv.append(q_bf * ez)
            k_lv.append(k_bf * ez)
        sel.append((srow // size == scol // size) & (srow % size >= half) & (scol % size < half))
        size *= 2
    qe_all = q * jnp.exp2(b)
    kl_all = kk * jnp.exp2(bcast_rows(b, CHUNK, CHUNK - 1) - b)

    b_last = [blk(b, h, c)[CHUNK - 1:CHUNK] for h, c in probs]
    qe = [blk(qe_all, h, c) for h, c in probs]
    vc = [blk(v, h, c) for h, c in probs]

    s_lv = [[_dot_nt(blk(ql, h, c), blk(kl, h, c)) for h, c in probs] for ql, kl in zip(q_lv, k_lv)]
    a = []
    for j in range(len(probs)):
        acc = jnp.zeros((CHUNK, CHUNK), F32)
        for lv in range(len(sel)):
            acc = jnp.where(sel[lv], s_lv[lv][j], acc)
        a.append(acc)
    u = [_dot_tn(x, blk(kl_all, h, c)) for x, (h, c) in zip(vc, probs)]
    qk = q * kk
    o_intra = [_dot(x, y) + jnp.sum(blk(qk, h, c), axis=-1, keepdims=True) * y
               for x, y, (h, c) in zip(a, vc, probs)]

    st = list(st)
    outs = [[None] * len(chunks) for _ in heads]
    for c in chunks:
        for h in heads:
            j = h * len(chunks) + c
            outs[h][c] = o_intra[j] + _dot_nt(qe[j], st[h])
            st[h] = st[h] * jnp.exp2(b_last[j]) + u[j]

    cols = []
    for h in heads:
        o = jnp.concatenate(outs[h], axis=0)
        cols.append(o * lax.rsqrt(jnp.mean(o * o, axis=-1, keepdims=True) + NORM_EPS))
    o = jnp.concatenate(cols, axis=1)
    return _bf(o * norm_w * _silu(x[:, 3 * w:])), st


def _in_hgrn2_kernel(x_ref, nw_ref, w_ref, lbl_ref, hnw_ref, mu_ref, ohg_ref, rw_ref, st_ref, prev_ref):
    @pl.when(pl.program_id(1) == 0)
    def _():
        st_ref[...] = jnp.zeros_like(st_ref)
        prev_ref[...] = jnp.zeros_like(prev_ref)

    hb = _bf(_rmsnorm(x_ref[...], nw_ref[...]))
    rw = jnp.dot(hb, w_ref[:, HG_COLS:], preferred_element_type=F32)
    hg = jnp.dot(hb, w_ref[:, :HG_COLS], preferred_element_type=F32)
    rw_shift = _shift_rows(rw, prev_ref[...], 1)
    prev_ref[...] = rw[rw.shape[0] - SUBLANES:]
    rw_ref[...] = rw + (rw_shift - rw) * mu_ref[...]

    logits = lbl_ref[...]
    e = jnp.exp(logits - jnp.max(logits, axis=0, keepdims=True))
    lb = e[0:1] / jnp.sum(e, axis=0, keepdims=True)

    st = [st_ref[h] for h in range(HG_HEADS)]
    for i in range(ROW_TILE // HG_TILE):
        rows = slice(i * HG_TILE, (i + 1) * HG_TILE)
        ohg_ref[rows, :], st = _hgrn2_rows(hg[rows], lb, hnw_ref[...], st)
    for h in range(HG_HEADS):
        st_ref[h] = st[h]


def _in_hgrn2(x2, norm_w, w_in_bf, lb_logits, hg_norm_w, rw_mu, batch, seq):
    m = x2.shape[0]
    nt = seq // ROW_TILE

    def rows(width):
        return pl.BlockSpec((ROW_TILE, width), lambda b, t: (b * nt + t, 0))

    def whole(shape):
        return pl.BlockSpec(shape, lambda b, t: (0, 0), pipeline_mode=pl.Buffered(1))

    return pl.pallas_call(
        _in_hgrn2_kernel,
        grid=(batch, nt),
        in_specs=[rows(D_MODEL), whole((1, D_MODEL)), whole((D_MODEL, IN_COLS)),
                  whole((2, HG_WIDTH)), whole((1, HG_WIDTH)), whole((1, RW_COLS))],
        out_specs=[rows(HG_WIDTH), rows(RW_COLS)],
        out_shape=[
            jax.ShapeDtypeStruct((m, HG_WIDTH), BF16),
            jax.ShapeDtypeStruct((m, RW_COLS), F32),
        ],
        scratch_shapes=[pltpu.VMEM((HG_HEADS, HG_HEAD_DIM, HG_HEAD_DIM), F32),
                        pltpu.VMEM((SUBLANES, RW_COLS), F32)],
        compiler_params=pltpu.CompilerParams(
            dimension_semantics=("arbitrary", "arbitrary"), vmem_limit_bytes=VMEM_LIMIT_BYTES),
        name="in_hgrn2",
    )(x2, norm_w, w_in_bf, lb_logits, hg_norm_w, rw_mu)


def _stack2(x):
    lane = _iota2(x.shape, 1)
    return jnp.concatenate([jnp.where(lane < RW_HEAD_DIM, x, 0.0),
                            jnp.where(lane >= RW_HEAD_DIM, x, 0.0)], axis=0)


def _rwkv7_kernel(x_ref, w0_ref, wwa_ref, a0_ref, g2_ref, kk_ref, ka_ref, rk_ref, lnw_ref, lnb_ref,
                  o_ref, z_ref):
    @pl.when(pl.program_id(1) == 0)
    def _():
        z_ref[...] = jnp.zeros_like(z_ref)

    w = RW_WIDTH
    lane = _iota2((LANES, LANES), 1)
    row = _iota2((LANES, LANES), 0)
    same_head = (row // RW_HEAD_DIM) == (lane // RW_HEAD_DIM)
    head_ones = jnp.where(same_head, 1.0, 0.0).astype(F32)

    def head_sums(t):
        return jnp.concatenate(
            [_dot_exact_lhs(t[:, p * LANES:(p + 1) * LANES], head_ones) for p in range(RW_PAIRS)], axis=1)

    wrow = _iota2((CHUNK, LANES), 0)
    wcol = _iota2((CHUNK, LANES), 1) % CHUNK
    strict = wcol < wrow
    incl = wcol <= wrow
    eye_w = jnp.where(wcol == wrow, 1.0, 0.0).astype(F32)
    eye = lane == row
    pairs = range(RW_PAIRS)

    def gates(rows):
        r, k, v = x_ref[rows, :w], x_ref[rows, w:2 * w], x_ref[rows, 2 * w:3 * w]
        wa = x_ref[rows, 3 * w:3 * w + LANES]
        gd = x_ref[rows, 3 * w + LANES:]
        n = r.shape[0]

        wa = jnp.where(_iota2(wa.shape, 1) < RW_DECAY_LORA, jnp.tanh(wa), wa)
        lora = _dot(wa, wwa_ref[...])
        zw = -(w0_ref[...] + lora[:, :w])
        softplus = jnp.maximum(zw, 0.0) + jnp.log(1.0 + jnp.exp(-jnp.abs(zw)))
        logw = -jnp.exp(-softplus - 0.5)
        a = jax.nn.sigmoid(a0_ref[...] + lora[:, w:])
        g = _dot(jax.nn.sigmoid(gd), g2_ref[...])

        kk = k * kk_ref[...]
        kk = kk / jnp.maximum(jnp.sqrt(head_sums(kk * kk)), L2_EPS)
        k = k * (1.0 + (a - 1.0) * ka_ref[...])
        beta = kk * a
        bonus = head_sums(r * k * rk_ref[...]) * v

        cum = _dot_exact_rhs(_chunk_tril(n), logw)
        e_neg = jnp.exp(-cum)
        return dict(n=n, cum=cum, a_t=-kk * jnp.exp(cum - logw), r_t=r * jnp.exp(cum), b_t=beta * e_neg,
                    k_t=k * e_neg, beta=beta, k=k, v=v, bonus=bonus, g=g)

    def recur(d, z):
        chunks = range(d["n"] // CHUNK)
        probs = [(p, c) for p in pairs for c in chunks]

        def blk(t, p, c):
            return t[c * CHUNK:(c + 1) * CHUNK, p * LANES:(p + 1) * LANES]

        c_last = [blk(d["cum"], p, c)[CHUNK - 1:CHUNK] for p, c in probs]
        e_last = [jnp.exp(cl - blk(d["cum"], p, c)) for (p, c), cl in zip(probs, c_last)]
        a_c = [blk(d["a_t"], p, c) for p, c in probs]
        r_c = [blk(d["r_t"], p, c) for p, c in probs]
        b_h = [blk(d["beta"], p, c) * e for (p, c), e in zip(probs, e_last)]
        k_h = [blk(d["k"], p, c) * e for (p, c), e in zip(probs, e_last)]
        vc = [blk(d["v"], p, c) for p, c in probs]
        v_st = [_stack2(t) for t in vc]

        s4 = [_dot_nt(jnp.concatenate([aa, rr], axis=0),
                      jnp.concatenate([_stack2(blk(d["b_t"], p, c)), _stack2(blk(d["k_t"], p, c))], axis=0))
              for aa, rr, (p, c) in zip(a_c, r_c, probs)]
        l_ab = [jnp.where(strict, s[:CHUNK, :LANES], 0.0) for s in s4]
        l_ak = [jnp.where(strict, s[:CHUNK, LANES:], 0.0) for s in s4]
        a_rb = [jnp.where(incl, s[CHUNK:, :LANES], 0.0) for s in s4]
        a_rk = [jnp.where(incl, s[CHUNK:, LANES:], 0.0) for s in s4]

        tm = [l + eye_w for l in l_ab]
        lp = [_dot(l, _stack2(l)) for l in l_ab]
        w1 = [_dot(l, t) for l, t in zip(l_ak, v_st)]
        for _ in range(4):
            both = [_dot(jnp.concatenate([t, l], axis=0), _stack2(l)) for t, l in zip(tm, lp)]
            tm = [t + bo[:CHUNK] for t, bo in zip(tm, both)]
            lp = [bo[CHUNK:] for bo in both]
        tm = [t + _dot(t, _stack2(l)) for t, l in zip(tm, lp)]

        mp = [_dot(t, jnp.concatenate([_stack2(aa), _stack2(ww)], axis=1))
              for t, aa, ww in zip(tm, a_c, w1)]
        ry = [_dot(jnp.concatenate([lb, lk], axis=1),
                   jnp.concatenate([jnp.concatenate([_stack2(t[:, :LANES]), _stack2(t[:, LANES:])], axis=1),
                                    jnp.concatenate([jnp.zeros_like(vs), vs], axis=1)], axis=0))
              for lb, lk, t, vs in zip(a_rb, a_rk, mp, v_st)]
        rq = [rr + t[:, :LANES] for rr, t in zip(r_c, ry)]
        y0 = [t[:, LANES:] for t in ry]
        gu = [_dot_tn(jnp.concatenate([bh, kh], axis=0),
                      jnp.concatenate([t, jnp.concatenate([jnp.zeros_like(vv), vv], axis=1)], axis=0))
              for bh, kh, t, vv in zip(b_h, k_h, mp, vc)]
        g_mat = [jnp.where(same_head, t[:, :LANES], 0.0) + jnp.where(eye, jnp.exp(cl), 0.0)
                 for t, cl in zip(gu, c_last)]
        u_mat = [jnp.where(same_head, t[:, LANES:], 0.0) for t in gu]

        z = list(z)
        ys = [[None] * len(chunks) for _ in pairs]
        for c in chunks:
            for p in pairs:
                j = p * len(chunks) + c
                out = _dot(jnp.concatenate([g_mat[j], rq[j]], axis=0), z[p])
                ys[p][c] = out[LANES:] + y0[j]
                z[p] = out[:LANES] + u_mat[j]
        return jnp.concatenate([jnp.concatenate(ys[p], axis=0) for p in pairs], axis=1), z

    def finish(y, d):
        inv_n = 1.0 / RW_HEAD_DIM
        mean = head_sums(y) * inv_n
        yc = y - mean
        var = head_sums(yc * yc) * inv_n
        y = yc * lax.rsqrt(var + RW_GN_EPS) * lnw_ref[...] + lnb_ref[...]
        return _bf((y + d["bonus"]) * d["g"])

    z = [z_ref[p] for p in pairs]
    pending = None
    for i in range(x_ref.shape[0] // RW_BLOCK):
        rows = slice(i * RW_BLOCK, (i + 1) * RW_BLOCK)
        d = gates(rows)
        y, z = recur(d, z)
        if pending is not None:
            o_ref[pending[0], :] = finish(*pending[1:])
        pending = (rows, y, d)
    o_ref[pending[0], :] = finish(*pending[1:])
    for p in pairs:
        z_ref[p] = z[p]


def _rwkv7(rw_p, w0, wwa_bf, a0, g2_bf, k_k, k_a, r_k, ln_w, ln_b, batch, seq):
    m = rw_p.shape[0]
    nt = seq // SCAN_TILE

    def vec(width):
        return pl.BlockSpec((1, width), lambda b, t: (0, 0))

    return pl.pallas_call(
        _rwkv7_kernel,
        grid=(batch, nt),
        in_specs=[
            pl.BlockSpec((SCAN_TILE, RW_COLS), lambda b, t: (b * nt + t, 0)),
            vec(RW_WIDTH),
            pl.BlockSpec((LANES, 2 * RW_WIDTH), lambda b, t: (0, 0)),
            vec(RW_WIDTH),
            pl.BlockSpec((RW_GATE_LORA, RW_WIDTH), lambda b, t: (0, 0)),
            vec(RW_WIDTH), vec(RW_WIDTH), vec(RW_WIDTH), vec(RW_WIDTH), vec(RW_WIDTH),
        ],
        out_specs=pl.BlockSpec((SCAN_TILE, RW_WIDTH), lambda b, t: (b * nt + t, 0)),
        out_shape=jax.ShapeDtypeStruct((m, RW_WIDTH), BF16),
        scratch_shapes=[pltpu.VMEM((RW_PAIRS, LANES, LANES), F32)],
        compiler_params=pltpu.CompilerParams(
            dimension_semantics=("arbitrary", "arbitrary"), vmem_limit_bytes=VMEM_LIMIT_BYTES),
        name="rwkv7",
    )(rw_p, w0, wwa_bf, a0, g2_bf, k_k, k_a, r_k, ln_w, ln_b)


def _mix_ffn_kernel(x_ref, ohg_ref, orw_ref, wo_ref, n2_ref, wu_ref, cw_ref, cb_ref, wd_ref, nf_ref,
                    o_ref, prev_ref, u0_ref, u1_ref, act_ref):
    u_refs = (u0_ref, u1_ref)
    hist = SUBLANES

    @pl.when(pl.program_id(1) == 0)
    def _():
        prev_ref[...] = jnp.zeros_like(prev_ref)

    x1 = (x_ref[...]
          + jnp.dot(ohg_ref[...], wo_ref[:HG_WIDTH], preferred_element_type=F32)
          + jnp.dot(orw_ref[...], wo_ref[HG_WIDTH:], preferred_element_type=F32))
    h = _bf(_rmsnorm(x1, n2_ref[...]))
    n = h.shape[0]

    ct = FFN_COL_TILE
    ntile = D_FF // ct

    def halves(j):
        return ((slice(0, ct), slice(j * ct, (j + 1) * ct)),
                (slice(ct, 2 * ct), slice(D_FF + j * ct, D_FF + (j + 1) * ct)))

    def up(j):
        u = u_refs[j % 2]
        for dst, cols in halves(j):
            u[0:hist, dst] = prev_ref[:, cols]
            u[hist:, dst] = jnp.dot(h, wu_ref[:, cols], preferred_element_type=F32)
            prev_ref[:, cols] = u[n:n + hist, dst]

    def conv_gate(j):
        u = u_refs[j % 2]
        c = []
        for dst, cols in halves(j):
            cw = cw_ref[:, cols]
            c.append(cw[2:3] * u[hist:n + hist, dst] + cw[1:2] * u[hist - 1:n + hist - 1, dst]
                     + cw[0:1] * u[hist - 2:n + hist - 2, dst] + cb_ref[:, cols])
        act_ref[:, j * ct:(j + 1) * ct] = _bf(_silu(c[0]) * c[1])

    up(0)
    for j in range(ntile):
        if j + 1 < ntile:
            up(j + 1)
        conv_gate(j)
    y = x1 + jnp.dot(act_ref[...], wd_ref[...], preferred_element_type=F32)
    o_ref[...] = _rmsnorm(y, nf_ref[...])


def _mix_ffn(x2, o_hg, o_rw, w_out_bf, norm2_w, w_up_bf, conv_w, conv_b, w_down_bf, final_w, batch, seq):
    m = x2.shape[0]
    nt = seq // ROW_TILE

    def rows(width):
        return pl.BlockSpec((ROW_TILE, width), lambda b, t: (b * nt + t, 0))

    def whole(shape):
        return pl.BlockSpec(shape, lambda b, t: (0, 0), pipeline_mode=pl.Buffered(1))

    return pl.pallas_call(
        _mix_ffn_kernel,
        grid=(batch, nt),
        in_specs=[
            rows(D_MODEL), rows(HG_WIDTH), rows(RW_WIDTH),
            whole((D_MODEL, D_MODEL)), whole((1, D_MODEL)),
            whole((D_MODEL, 2 * D_FF)), whole((3, 2 * D_FF)), whole((1, 2 * D_FF)),
            whole((D_FF, D_MODEL)), whole((1, D_MODEL)),
        ],
        out_specs=rows(D_MODEL),
        out_shape=jax.ShapeDtypeStruct((m, D_MODEL), F32),
        scratch_shapes=[pltpu.VMEM((SUBLANES, 2 * D_FF), F32),
                        pltpu.VMEM((SUBLANES + ROW_TILE, 2 * FFN_COL_TILE), F32),
                        pltpu.VMEM((SUBLANES + ROW_TILE, 2 * FFN_COL_TILE), F32),
                        pltpu.VMEM((ROW_TILE, D_FF), BF16)],
        compiler_params=pltpu.CompilerParams(
            dimension_semantics=("arbitrary", "arbitrary"), vmem_limit_bytes=VMEM_LIMIT_BYTES),
        name="mix_ffn",
    )(x2, o_hg, o_rw, w_out_bf, norm2_w, w_up_bf, conv_w, conv_b, w_down_bf, final_w)


def kernel(x, norm1_w, w_in, hg_lb_logits, hg_norm_w, rw_shift_mu, rw_w0, rw_w2, rw_a0, rw_a2, rw_g2,
           rw_k_k, rw_k_a, rw_r_k, rw_ln_w, rw_ln_b, w_out, norm2_w, w_up, conv_w, conv_b, w_down,
           final_norm_w):
    batch, seq, _ = x.shape
    assert seq % SCAN_TILE == 0 and seq % ROW_TILE == 0 and norm1_w.shape[0] == 1
    assert CHUNK == RW_HEAD_DIM and 2 * RW_HEAD_DIM == LANES
    x2 = x.reshape(batch * seq, D_MODEL)

    zeros = jnp.zeros_like(rw_w2[0])
    wwa = _bf(jnp.concatenate([jnp.concatenate([rw_w2[0], zeros], axis=1),
                               jnp.concatenate([zeros, rw_a2[0]], axis=1)], axis=0))

    o_hg, rw_p = _in_hgrn2(x2, norm1_w, _bf(w_in[0]), hg_lb_logits, hg_norm_w, rw_shift_mu, batch, seq)
    o_rw = _rwkv7(rw_p, rw_w0, wwa, rw_a0, _bf(rw_g2[0]), rw_k_k, rw_k_a, rw_r_k, rw_ln_w, rw_ln_b, batch, seq)
    out = _mix_ffn(x2, o_hg, o_rw, _bf(w_out[0]), norm2_w, _bf(w_up[0]), conv_w[0], conv_b, _bf(w_down[0]),
                   final_norm_w.reshape(1, D_MODEL), batch, seq)
    return out.reshape(batch, seq, D_MODEL)
```

```python
import jax
import jax.numpy as jnp
from jax import lax
from jax.experimental import pallas as pl
from jax.experimental.pallas import tpu as pltpu

F32 = jnp.float32
BF16 = jnp.bfloat16

D_MODEL = 1024
HG_WIDTH = 512
HG_HEAD_DIM = 128
HG_HEADS = HG_WIDTH // HG_HEAD_DIM
RW_WIDTH = 512
RW_HEAD_DIM = 64
RW_DECAY_LORA = 64
RW_AAA_LORA = 64
RW_GATE_LORA = 128
RW_COLS = 3 * RW_WIDTH + RW_DECAY_LORA + RW_AAA_LORA + RW_GATE_LORA
HG_COLS = 4 * HG_WIDTH
IN_COLS = HG_COLS + RW_COLS
D_FF = 2816
NORM_EPS = 1e-6
RW_GN_EPS = 64e-5
L2_EPS = 1e-12

LANES = 128
SUBLANES = 8
CHUNK = 64
RW_PAIRS = RW_WIDTH // LANES
VMEM_LIMIT_BYTES = 56 * 1024 * 1024

ROW_TILE = 512
SCAN_TILE = 1024
RW_BLOCK = 256
HG_TILE = 256
FFN_COL_TILE = 256


def _bf(x):
    return x.astype(BF16)


def _dot(a, b):
    return jnp.dot(_bf(a), _bf(b), preferred_element_type=F32)


def _dot_nt(a, b):
    return lax.dot_general(_bf(a), _bf(b), (((1,), (1,)), ((), ())), preferred_element_type=F32)


def _dot_tn(a, b):
    return jnp.dot(_bf(a.T), _bf(b), preferred_element_type=F32)


def _split3(x):
    h1 = _bf(x)
    r1 = x - h1.astype(F32)
    h2 = _bf(r1)
    r2 = r1 - h2.astype(F32)
    return h1, h2, _bf(r2)


def _dot_exact_rhs(m01, x):
    w = x.shape[1]
    out = jnp.dot(_bf(m01), jnp.concatenate(_split3(x), axis=1), preferred_element_type=F32)
    return out[:, :w] + out[:, w:2 * w] + out[:, 2 * w:]


def _dot_exact_lhs(x, m01):
    n = x.shape[0]
    out = jnp.dot(jnp.concatenate(_split3(x), axis=0), _bf(m01), preferred_element_type=F32)
    return out[:n] + out[n:2 * n] + out[2 * n:]


def _iota2(shape, dim):
    return lax.broadcasted_iota(jnp.int32, shape, dim)


def _chunk_tril(n):
    r = _iota2((n, n), 0)
    c = _iota2((n, n), 1)
    return jnp.where((r // CHUNK == c // CHUNK) & (c <= r), 1.0, 0.0).astype(F32)


def _shift_rows(x, prev_rows, k):
    rolled = pltpu.roll(x, k, axis=0)
    prev = pltpu.roll(prev_rows, k, axis=0)
    head = jnp.where(_iota2(prev.shape, 0) < k, prev, rolled[:SUBLANES])
    return jnp.concatenate([head, rolled[SUBLANES:]], axis=0)


def _silu(x):
    return x * jax.nn.sigmoid(x)


def _rmsnorm(x, w):
    return x * lax.rsqrt(jnp.mean(x * x, axis=-1, keepdims=True) + NORM_EPS) * w


def _hgrn2_rows(x, lb, norm_w, st):
    w = HG_WIDTH
    f = lb + (1.0 - lb) * jax.nn.sigmoid(x[:, w:2 * w])
    kk = 1.0 - f
    q = _silu(x[:, :w]) * (HG_HEAD_DIM ** -0.5)
    v = x[:, 2 * w:3 * w]
    n = q.shape[0]
    b = _dot_exact_rhs(_chunk_tril(n), jnp.log2(f))

    heads = range(HG_HEADS)
    chunks = range(n // CHUNK)
    probs = [(h, c) for h in heads for c in chunks]

    def blk(x, h, c):
        return x[c * CHUNK:(c + 1) * CHUNK, h * LANES:(h + 1) * LANES]

    def bcast_rows(x, period, offset):
        return jnp.concatenate(
            [jnp.broadcast_to(x[g * period + offset:g * period + offset + 1], (period, x.shape[1]))
             for g in range(x.shape[0] // period)], axis=0)

    t_idx = _iota2(b.shape, 0)
    srow = _iota2((CHUNK, CHUNK), 0)
    scol = _iota2((CHUNK, CHUNK), 1)
    q_bf, k_bf = _bf(q), _bf(kk)
    q_lv, k_lv, sel = [_bf(q * f)], [k_bf], []
    size = 2
    while size <= CHUNK:
        half = size // 2
        if size == 4:
            pos = t_idx % size
            m = jnp.where(pos == 0, pltpu.roll(b, n - 1, axis=0),
                          jnp.where(pos == 1, b,
                                    jnp.where(pos == 2, pltpu.roll(b, 1, axis=0), pltpu.roll(b, 2, axis=0))))
        elif size > 4:
            m = bcast_rows(b, size, half - 1)
        if size > 2:
            ez = _bf(jnp.exp2(-jnp.abs(b - m)))
            q_lv.append(q_bf * ez)
            k_lv.append(k_bf * ez)
        sel.append((srow // size == scol // size) & (srow % size >= half) & (scol % size < half))
        size *= 2
    qe_all = q * jnp.exp2(b)
    kl_all = kk * jnp.exp2(bcast_rows(b, CHUNK, CHUNK - 1) - b)

    b_last = [blk(b, h, c)[CHUNK - 1:CHUNK] for h, c in probs]
    qe = [blk(qe_all, h, c) for h, c in probs]
    vc = [blk(v, h, c) for h, c in probs]

    s_lv = [[_dot_nt(blk(ql, h, c), blk(kl, h, c)) for h, c in probs] for ql, kl in zip(q_lv, k_lv)]
    a = []
    for j in range(len(probs)):
        acc = jnp.zeros((CHUNK, CHUNK), F32)
        for lv in range(len(sel)):
            acc = jnp.where(sel[lv], s_lv[lv][j], acc)
        a.append(acc)
    u = [_dot_tn(x, blk(kl_all, h, c)) for x, (h, c) in zip(vc, probs)]
    qk = q * kk
    o_intra = [_dot(x, y) + jnp.sum(blk(qk, h, c), axis=-1, keepdims=True) * y
               for x, y, (h, c) in zip(a, vc, probs)]

    st = list(st)
    outs = [[None] * len(chunks) for _ in heads]
    for c in chunks:
        for h in heads:
            j = h * len(chunks) + c
            outs[h][c] = o_intra[j] + _dot_nt(qe[j], st[h])
            st[h] = st[h] * jnp.exp2(b_last[j]) + u[j]

    cols = []
    for h in heads:
        o = jnp.concatenate(outs[h], axis=0)
        cols.append(o * lax.rsqrt(jnp.mean(o * o, axis=-1, keepdims=True) + NORM_EPS))
    o = jnp.concatenate(cols, axis=1)
    return _bf(o * norm_w * _silu(x[:, 3 * w:])), st


def _in_hgrn2_kernel(x_ref, nw_ref, w_ref, lbl_ref, hnw_ref, mu_ref, ohg_ref, rw_ref, st_ref, prev_ref):
    @pl.when(pl.program_id(1) == 0)
    def _():
        st_ref[...] = jnp.zeros_like(st_ref)
        prev_ref[...] = jnp.zeros_like(prev_ref)

    hb = _bf(_rmsnorm(x_ref[...], nw_ref[...]))
    rw = jnp.dot(hb, w_ref[:, HG_COLS:], preferred_element_type=F32)
    hg = jnp.dot(hb, w_ref[:, :HG_COLS], preferred_element_type=F32)
    rw_shift = _shift_rows(rw, prev_ref[...], 1)
    prev_ref[...] = rw[rw.shape[0] - SUBLANES:]
    rw_ref[...] = rw + (rw_shift - rw) * mu_ref[...]

    logits = lbl_ref[...]
    e = jnp.exp(logits - jnp.max(logits, axis=0, keepdims=True))
    lb = e[0:1] / jnp.sum(e, axis=0, keepdims=True)

    st = [st_ref[h] for h in range(HG_HEADS)]
    for i in range(ROW_TILE // HG_TILE):
        rows = slice(i * HG_TILE, (i + 1) * HG_TILE)
        ohg_ref[rows, :], st = _hgrn2_rows(hg[rows], lb, hnw_ref[...], st)
    for h in range(HG_HEADS):
        st_ref[h] = st[h]


def _in_hgrn2(x2, norm_w, w_in_bf, lb_logits, hg_norm_w, rw_mu, batch, seq):
    m = x2.shape[0]
    nt = seq // ROW_TILE

    def rows(width):
        return pl.BlockSpec((ROW_TILE, width), lambda b, t: (b * nt + t, 0))

    def whole(shape):
        return pl.BlockSpec(shape, lambda b, t: (0, 0), pipeline_mode=pl.Buffered(1))

    return pl.pallas_call(
        _in_hgrn2_kernel,
        grid=(batch, nt),
        in_specs=[rows(D_MODEL), whole((1, D_MODEL)), whole((D_MODEL, IN_COLS)),
                  whole((2, HG_WIDTH)), whole((1, HG_WIDTH)), whole((1, RW_COLS))],
        out_specs=[rows(HG_WIDTH), rows(RW_COLS)],
        out_shape=[
            jax.ShapeDtypeStruct((m, HG_WIDTH), BF16),
            jax.ShapeDtypeStruct((m, RW_COLS), F32),
        ],
        scratch_shapes=[pltpu.VMEM((HG_HEADS, HG_HEAD_DIM, HG_HEAD_DIM), F32),
                        pltpu.VMEM((SUBLANES, RW_COLS), F32)],
        compiler_params=pltpu.CompilerParams(
            dimension_semantics=("arbitrary", "arbitrary"), vmem_limit_bytes=VMEM_LIMIT_BYTES),
        name="in_hgrn2",
    )(x2, norm_w, w_in_bf, lb_logits, hg_norm_w, rw_mu)


def _stack2(x):
    lane = _iota2(x.shape, 1)
    return jnp.concatenate([jnp.where(lane < RW_HEAD_DIM, x, 0.0),
                            jnp.where(lane >= RW_HEAD_DIM, x, 0.0)], axis=0)


def _rwkv7_kernel(x_ref, w0_ref, wwa_ref, a0_ref, g2_ref, kk_ref, ka_ref, rk_ref, lnw_ref, lnb_ref,
                  o_ref, z_ref):
    @pl.when(pl.program_id(1) == 0)
    def _():
        z_ref[...] = jnp.zeros_like(z_ref)

    w = RW_WIDTH
    lane = _iota2((LANES, LANES), 1)
    row = _iota2((LANES, LANES), 0)
    same_head = (row // RW_HEAD_DIM) == (lane // RW_HEAD_DIM)
    head_ones = jnp.where(same_head, 1.0, 0.0).astype(F32)

    def head_sums(t):
        return jnp.concatenate(
            [_dot_exact_lhs(t[:, p * LANES:(p + 1) * LANES], head_ones) for p in range(RW_PAIRS)], axis=1)

    wrow = _iota2((CHUNK, LANES), 0)
    wcol = _iota2((CHUNK, LANES), 1) % CHUNK
    strict = wcol < wrow
    incl = wcol <= wrow
    eye_w = jnp.where(wcol == wrow, 1.0, 0.0).astype(F32)
    eye = lane == row
    pairs = range(RW_PAIRS)

    def gates(rows):
        r, k, v = x_ref[rows, :w], x_ref[rows, w:2 * w], x_ref[rows, 2 * w:3 * w]
        wa = x_ref[rows, 3 * w:3 * w + LANES]
        gd = x_ref[rows, 3 * w + LANES:]
        n = r.shape[0]

        wa = jnp.where(_iota2(wa.shape, 1) < RW_DECAY_LORA, jnp.tanh(wa), wa)
        lora = _dot(wa, wwa_ref[...])
        zw = -(w0_ref[...] + lora[:, :w])
        softplus = jnp.maximum(zw, 0.0) + jnp.log(1.0 + jnp.exp(-jnp.abs(zw)))
        logw = -jnp.exp(-softplus - 0.5)
        a = jax.nn.sigmoid(a0_ref[...] + lora[:, w:])
        g = _dot(jax.nn.sigmoid(gd), g2_ref[...])

        kk = k * kk_ref[...]
        kk = kk / jnp.maximum(jnp.sqrt(head_sums(kk * kk)), L2_EPS)
        k = k * (1.0 + (a - 1.0) * ka_ref[...])
        beta = kk * a
        bonus = head_sums(r * k * rk_ref[...]) * v

        cum = _dot_exact_rhs(_chunk_tril(n), logw)
        e_neg = jnp.exp(-cum)
        return dict(n=n, cum=cum, a_t=-kk * jnp.exp(cum - logw), r_t=r * jnp.exp(cum), b_t=beta * e_neg,
                    k_t=k * e_neg, beta=beta, k=k, v=v, bonus=bonus, g=g)

    def recur(d, z):
        chunks = range(d["n"] // CHUNK)
        probs = [(p, c) for p in pairs for c in chunks]

        def blk(t, p, c):
            return t[c * CHUNK:(c + 1) * CHUNK, p * LANES:(p + 1) * LANES]

        c_last = [blk(d["cum"], p, c)[CHUNK - 1:CHUNK] for p, c in probs]
        e_last = [jnp.exp(cl - blk(d["cum"], p, c)) for (p, c), cl in zip(probs, c_last)]
        a_c = [blk(d["a_t"], p, c) for p, c in probs]
        r_c = [blk(d["r_t"], p, c) for p, c in probs]
        b_h = [blk(d["beta"], p, c) * e for (p, c), e in zip(probs, e_last)]
        k_h = [blk(d["k"], p, c) * e for (p, c), e in zip(probs, e_last)]
        vc = [blk(d["v"], p, c) for p, c in probs]
        v_st = [_stack2(t) for t in vc]

        s4 = [_dot_nt(jnp.concatenate([aa, rr], axis=0),
                      jnp.concatenate([_stack2(blk(d["b_t"], p, c)), _stack2(blk(d["k_t"], p, c))], axis=0))
              for aa, rr, (p, c) in zip(a_c, r_c, probs)]
        l_ab = [jnp.where(strict, s[:CHUNK, :LANES], 0.0) for s in s4]
        l_ak = [jnp.where(strict, s[:CHUNK, LANES:], 0.0) for s in s4]
        a_rb = [jnp.where(incl, s[CHUNK:, :LANES], 0.0) for s in s4]
        a_rk = [jnp.where(incl, s[CHUNK:, LANES:], 0.0) for s in s4]

        tm = [l + eye_w for l in l_ab]
        lp = [_dot(l, _stack2(l)) for l in l_ab]
        w1 = [_dot(l, t) for l, t in zip(l_ak, v_st)]
        for _ in range(4):
            both = [_dot(jnp.concatenate([t, l], axis=0), _stack2(l)) for t, l in zip(tm, lp)]
            tm = [t + bo[:CHUNK] for t, bo in zip(tm, both)]
            lp = [bo[CHUNK:] for bo in both]
        tm = [t + _dot(t, _stack2(l)) for t, l in zip(tm, lp)]

        mp = [_dot(t, jnp.concatenate([_stack2(aa), _stack2(ww)], axis=1))
              for t, aa, ww in zip(tm, a_c, w1)]
        ry = [_dot(jnp.concatenate([lb, lk], axis=1),
                   jnp.concatenate([jnp.concatenate([_stack2(t[:, :LANES]), _stack2(t[:, LANES:])], axis=1),
                                    jnp.concatenate([jnp.zeros_like(vs), vs], axis=1)], axis=0))
              for lb, lk, t, vs in zip(a_rb, a_rk, mp, v_st)]
        rq = [rr + t[:, :LANES] for rr, t in zip(r_c, ry)]
        y0 = [t[:, LANES:] for t in ry]
        gu = [_dot_tn(jnp.concatenate([bh, kh], axis=0),
                      jnp.concatenate([t, jnp.concatenate([jnp.zeros_like(vv), vv], axis=1)], axis=0))
              for bh, kh, t, vv in zip(b_h, k_h, mp, vc)]
        g_mat = [jnp.where(same_head, t[:, :LANES], 0.0) + jnp.where(eye, jnp.exp(cl), 0.0)
                 for t, cl in zip(gu, c_last)]
        u_mat = [jnp.where(same_head, t[:, LANES:], 0.0) for t in gu]

        z = list(z)
        ys = [[None] * len(chunks) for _ in pairs]
        for c in chunks:
            for p in pairs:
                j = p * len(chunks) + c
                out = _dot(jnp.concatenate([g_mat[j], rq[j]], axis=0), z[p])
                ys[p][c] = out[LANES:] + y0[j]
                z[p] = out[:LANES] + u_mat[j]
        return jnp.concatenate([jnp.concatenate(ys[p], axis=0) for p in pairs], axis=1), z

    def finish(y, d):
        inv_n = 1.0 / RW_HEAD_DIM
        mean = head_sums(y) * inv_n
        yc = y - mean
        var = head_sums(yc * yc) * inv_n
        y = yc * lax.rsqrt(var + RW_GN_EPS) * lnw_ref[...] + lnb_ref[...]
        return _bf((y + d["bonus"]) * d["g"])

    z = [z_ref[p] for p in pairs]
    pending = None
    for i in range(x_ref.shape[0] // RW_BLOCK):
        rows = slice(i * RW_BLOCK, (i + 1) * RW_BLOCK)
        d = gates(rows)
        y, z = recur(d, z)
        if pending is not None:
            o_ref[pending[0], :] = finish(*pending[1:])
        pending = (rows, y, d)
    o_ref[pending[0], :] = finish(*pending[1:])
    for p in pairs:
        z_ref[p] = z[p]


def _rwkv7(rw_p, w0, wwa_bf, a0, g2_bf, k_k, k_a, r_k, ln_w, ln_b, batch, seq):
    m = rw_p.shape[0]
    nt = seq // SCAN_TILE

    def vec(width):
        return pl.BlockSpec((1, width), lambda b, t: (0, 0))

    return pl.pallas_call(
        _rwkv7_kernel,
        grid=(batch, nt),
        in_specs=[
            pl.BlockSpec((SCAN_TILE, RW_COLS), lambda b, t: (b * nt + t, 0)),
            vec(RW_WIDTH),
            pl.BlockSpec((LANES, 2 * RW_WIDTH), lambda b, t: (0, 0)),
            vec(RW_WIDTH),
            pl.BlockSpec((RW_GATE_LORA, RW_WIDTH), lambda b, t: (0, 0)),
            vec(RW_WIDTH), vec(RW_WIDTH), vec(RW_WIDTH), vec(RW_WIDTH), vec(RW_WIDTH),
        ],
        out_specs=pl.BlockSpec((SCAN_TILE, RW_WIDTH), lambda b, t: (b * nt + t, 0)),
        out_shape=jax.ShapeDtypeStruct((m, RW_WIDTH), BF16),
        scratch_shapes=[pltpu.VMEM((RW_PAIRS, LANES, LANES), F32)],
        compiler_params=pltpu.CompilerParams(
            dimension_semantics=("arbitrary", "arbitrary"), vmem_limit_bytes=VMEM_LIMIT_BYTES),
        name="rwkv7",
    )(rw_p, w0, wwa_bf, a0, g2_bf, k_k, k_a, r_k, ln_w, ln_b)


def _mix_ffn_kernel(x_ref, ohg_ref, orw_ref, wo_ref, n2_ref, wu_ref, cw_ref, cb_ref, wd_ref, nf_ref,
                    o_ref, prev_ref, u0_ref, u1_ref, act_ref):
    u_refs = (u0_ref, u1_ref)
    hist = SUBLANES

    @pl.when(pl.program_id(1) == 0)
    def _():
        prev_ref[...] = jnp.zeros_like(prev_ref)

    x1 = (x_ref[...]
          + jnp.dot(ohg_ref[...], wo_ref[:HG_WIDTH], preferred_element_type=F32)
          + jnp.dot(orw_ref[...], wo_ref[HG_WIDTH:], preferred_element_type=F32))
    h = _bf(_rmsnorm(x1, n2_ref[...]))
    n = h.shape[0]

    ct = FFN_COL_TILE
    ntile = D_FF // ct

    def halves(j):
        return ((slice(0, ct), slice(j * ct, (j + 1) * ct)),
                (slice(ct, 2 * ct), slice(D_FF + j * ct, D_FF + (j + 1) * ct)))

    def up(j):
        u = u_refs[j % 2]
        for dst, cols in halves(j):
            u[0:hist, dst] = prev_ref[:, cols]
            u[hist:, dst] = jnp.dot(h, wu_ref[:, cols], preferred_element_type=F32)
            prev_ref[:, cols] = u[n:n + hist, dst]

    def conv_gate(j):
        u = u_refs[j % 2]
        c = []
        for dst, cols in halves(j):
            cw = cw_ref[:, cols]
            c.append(cw[2:3] * u[hist:n + hist, dst] + cw[1:2] * u[hist - 1:n + hist - 1, dst]
                     + cw[0:1] * u[hist - 2:n + hist - 2, dst] + cb_ref[:, cols])
        act_ref[:, j * ct:(j + 1) * ct] = _bf(_silu(c[0]) * c[1])

    up(0)
    for j in range(ntile):
        if j + 1 < ntile:
            up(j + 1)
        conv_gate(j)
    y = x1 + jnp.dot(act_ref[...], wd_ref[...], preferred_element_type=F32)
    o_ref[...] = _rmsnorm(y, nf_ref[...])


def _mix_ffn(x2, o_hg, o_rw, w_out_bf, norm2_w, w_up_bf, conv_w, conv_b, w_down_bf, final_w, batch, seq):
    m = x2.shape[0]
    nt = seq // ROW_TILE

    def rows(width):
        return pl.BlockSpec((ROW_TILE, width), lambda b, t: (b * nt + t, 0))

    def whole(shape):
        return pl.BlockSpec(shape, lambda b, t: (0, 0), pipeline_mode=pl.Buffered(1))

    return pl.pallas_call(
        _mix_ffn_kernel,
        grid=(batch, nt),
        in_specs=[
            rows(D_MODEL), rows(HG_WIDTH), rows(RW_WIDTH),
            whole((D_MODEL, D_MODEL)), whole((1, D_MODEL)),
            whole((D_MODEL, 2 * D_FF)), whole((3, 2 * D_FF)), whole((1, 2 * D_FF)),
            whole((D_FF, D_MODEL)), whole((1, D_MODEL)),
        ],
        out_specs=rows(D_MODEL),
        out_shape=jax.ShapeDtypeStruct((m, D_MODEL), F32),
        scratch_shapes=[pltpu.VMEM((SUBLANES, 2 * D_FF), F32),
                        pltpu.VMEM((SUBLANES + ROW_TILE, 2 * FFN_COL_TILE), F32),
                        pltpu.VMEM((SUBLANES + ROW_TILE, 2 * FFN_COL_TILE), F32),
                        pltpu.VMEM((ROW_TILE, D_FF), BF16)],
        compiler_params=pltpu.CompilerParams(
            dimension_semantics=("arbitrary", "arbitrary"), vmem_limit_bytes=VMEM_LIMIT_BYTES),
        name="mix_ffn",
    )(x2, o_hg, o_rw, w_out_bf, norm2_w, w_up_bf, conv_w, conv_b, w_down_bf, final_w)


def kernel(x, norm1_w, w_in, hg_lb_logits, hg_norm_w, rw_shift_mu, rw_w0, rw_w2, rw_a0, rw_a2, rw_g2,
           rw_k_k, rw_k_a, rw_r_k, rw_ln_w, rw_ln_b, w_out, norm2_w, w_up, conv_w, conv_b, w_down,
           final_norm_w):
    batch, seq, _ = x.shape
    assert seq % SCAN_TILE == 0 and seq % ROW_TILE == 0 and norm1_w.shape[0] == 1
    assert CHUNK == RW_HEAD_DIM and 2 * RW_HEAD_DIM == LANES
    x2 = x.reshape(batch * seq, D_MODEL)

    zeros = jnp.zeros_like(rw_w2[0])
    wwa = _bf(jnp.concatenate([jnp.concatenate([rw_w2[0], zeros], axis=1),
                               jnp.concatenate([zeros, rw_a2[0]], axis=1)], axis=0))

    o_hg, rw_p = _in_hgrn2(x2, norm1_w, _bf(w_in[0]), hg_lb_logits, hg_norm_w, rw_shift_mu, batch, seq)
    o_rw = _rwkv7(rw_p, rw_w0, wwa, rw_a0, _bf(rw_g2[0]), rw_k_k, rw_k_a, rw_r_k, rw_ln_w, rw_ln_b, batch, seq)
    out = _mix_ffn(x2, o_hg, o_rw, _bf(w_out[0]), norm2_w, _bf(w_up[0]), conv_w[0], conv_b, _bf(w_down[0]),
                   final_norm_w.reshape(1, D_MODEL), batch, seq)
    return out.reshape(batch, seq, D_MODEL)
```

```python
import jax
import jax.numpy as jnp
from jax import lax
from jax.experimental import pallas as pl
from jax.experimental.pallas import tpu as pltpu

F32 = jnp.float32
BF16 = jnp.bfloat16

D_MODEL = 1024
HG_WIDTH = 512
HG_HEAD_DIM = 128
HG_HEADS = HG_WIDTH // HG_HEAD_DIM
RW_WIDTH = 512
RW_HEAD_DIM = 64
RW_DECAY_LORA = 64
RW_AAA_LORA = 64
RW_GATE_LORA = 128
RW_COLS = 3 * RW_WIDTH + RW_DECAY_LORA + RW_AAA_LORA + RW_GATE_LORA
HG_COLS = 4 * HG_WIDTH
IN_COLS = HG_COLS + RW_COLS
D_FF = 2816
NORM_EPS = 1e-6
RW_GN_EPS = 64e-5
L2_EPS = 1e-12

LANES = 128
SUBLANES = 8
CHUNK = 64
RW_PAIRS = RW_WIDTH // LANES
VMEM_LIMIT_BYTES = 56 * 1024 * 1024

ROW_TILE = 512
SCAN_TILE = 512
RW_BLOCK = 256
HG_TILE = 256
FFN_COL_TILE = 256


def _bf(x):
    return x.astype(BF16)


def _dot(a, b):
    return jnp.dot(_bf(a), _bf(b), preferred_element_type=F32)


def _dot_nt(a, b):
    return lax.dot_general(_bf(a), _bf(b), (((1,), (1,)), ((), ())), preferred_element_type=F32)


def _dot_tn(a, b):
    return jnp.dot(_bf(a.T), _bf(b), preferred_element_type=F32)


def _split3(x):
    h1 = _bf(x)
    r1 = x - h1.astype(F32)
    h2 = _bf(r1)
    r2 = r1 - h2.astype(F32)
    return h1, h2, _bf(r2)


def _dot_exact_rhs(m01, x):
    w = x.shape[1]
    out = jnp.dot(_bf(m01), jnp.concatenate(_split3(x), axis=1), preferred_element_type=F32)
    return out[:, :w] + out[:, w:2 * w] + out[:, 2 * w:]


def _dot_hilo_lhs(x, m01):
    n = x.shape[0]
    hi = _bf(x)
    lo = _bf(x - hi.astype(F32))
    out = jnp.dot(jnp.concatenate([hi, lo], axis=0), _bf(m01), preferred_element_type=F32)
    return out[:n] + out[n:]


def _iota2(shape, dim):
    return lax.broadcasted_iota(jnp.int32, shape, dim)


def _chunk_tril(n):
    r = _iota2((n, n), 0)
    c = _iota2((n, n), 1)
    return jnp.where((r // CHUNK == c // CHUNK) & (c <= r), 1.0, 0.0).astype(F32)


def _shift_rows(x, prev_rows, k):
    rolled = pltpu.roll(x, k, axis=0)
    prev = pltpu.roll(prev_rows, k, axis=0)
    head = jnp.where(_iota2(prev.shape, 0) < k, prev, rolled[:SUBLANES])
    return jnp.concatenate([head, rolled[SUBLANES:]], axis=0)


def _silu(x):
    return x * jax.nn.sigmoid(x)


def _rmsnorm(x, w):
    return x * lax.rsqrt(jnp.mean(x * x, axis=-1, keepdims=True) + NORM_EPS) * w


def _hgrn2_rows(x, lb, norm_w, st):
    w = HG_WIDTH
    f = lb + (1.0 - lb) * jax.nn.sigmoid(x[:, w:2 * w])
    kk = 1.0 - f
    q = _silu(x[:, :w]) * (HG_HEAD_DIM ** -0.5)
    v = x[:, 2 * w:3 * w]
    n = q.shape[0]
    b = _dot_exact_rhs(_chunk_tril(n), jnp.log2(f))

    heads = range(HG_HEADS)
    chunks = range(n // CHUNK)
    probs = [(h, c) for h in heads for c in chunks]

    def blk(x, h, c):
        return x[c * CHUNK:(c + 1) * CHUNK, h * LANES:(h + 1) * LANES]

    def bcast_rows(x, period, offset):
        return jnp.concatenate(
            [jnp.broadcast_to(x[g * period + offset:g * period + offset + 1], (period, x.shape[1]))
             for g in range(x.shape[0] // period)], axis=0)

    t_idx = _iota2(b.shape, 0)
    srow = _iota2((CHUNK, CHUNK), 0)
    scol = _iota2((CHUNK, CHUNK), 1)
    q_bf, k_bf = _bf(q), _bf(kk)
    q_lv, k_lv, sel = [_bf(q * f)], [k_bf], []
    size = 2
    while size <= CHUNK:
        half = size // 2
        if size == 4:
            pos = t_idx % size
            m = jnp.where(pos == 0, pltpu.roll(b, n - 1, axis=0),
                          jnp.where(pos == 1, b,
                                    jnp.where(pos == 2, pltpu.roll(b, 1, axis=0), pltpu.roll(b, 2, axis=0))))
        elif size > 4:
            m = bcast_rows(b, size, half - 1)
        if size > 2:
            ez = _bf(jnp.exp2(-jnp.abs(b - m)))
            q_lv.append(q_bf * ez)
            k_lv.append(k_bf * ez)
        sel.append((srow // size == scol // size) & (srow % size >= half) & (scol % size < half))
        size *= 2
    qe_all = q * jnp.exp2(b)
    kl_all = kk * jnp.exp2(bcast_rows(b, CHUNK, CHUNK - 1) - b)

    b_last = [blk(b, h, c)[CHUNK - 1:CHUNK] for h, c in probs]
    qe = [blk(qe_all, h, c) for h, c in probs]
    vc = [blk(v, h, c) for h, c in probs]

    s_lv = [[_dot_nt(blk(ql, h, c), blk(kl, h, c)) for h, c in probs] for ql, kl in zip(q_lv, k_lv)]
    a = []
    for j in range(len(probs)):
        acc = jnp.zeros((CHUNK, CHUNK), F32)
        for lv in range(len(sel)):
            acc = jnp.where(sel[lv], s_lv[lv][j], acc)
        a.append(acc)
    u = [_dot_tn(x, blk(kl_all, h, c)) for x, (h, c) in zip(vc, probs)]
    qk = q * kk
    o_intra = [_dot(x, y) + jnp.sum(blk(qk, h, c), axis=-1, keepdims=True) * y
               for x, y, (h, c) in zip(a, vc, probs)]

    st = list(st)
    outs = [[None] * len(chunks) for _ in heads]
    for c in chunks:
        for h in heads:
            j = h * len(chunks) + c
            outs[h][c] = o_intra[j] + _dot_nt(qe[j], st[h])
            st[h] = st[h] * jnp.exp2(b_last[j]) + u[j]

    cols = []
    for h in heads:
        o = jnp.concatenate(outs[h], axis=0)
        cols.append(o * lax.rsqrt(jnp.mean(o * o, axis=-1, keepdims=True) + NORM_EPS))
    o = jnp.concatenate(cols, axis=1)
    return _bf(o * norm_w * _silu(x[:, 3 * w:])), st


def _in_hgrn2_kernel(x_ref, nw_ref, w_ref, lbl_ref, hnw_ref, mu_ref, ohg_ref, rw_ref, st_ref, prev_ref):
    @pl.when(pl.program_id(1) == 0)
    def _():
        st_ref[...] = jnp.zeros_like(st_ref)
        prev_ref[...] = jnp.zeros_like(prev_ref)

    hb = _bf(_rmsnorm(x_ref[...], nw_ref[...]))
    rw = jnp.dot(hb, w_ref[:, HG_COLS:], preferred_element_type=F32)
    hg = jnp.dot(hb, w_ref[:, :HG_COLS], preferred_element_type=F32)
    rw_shift = _shift_rows(rw, prev_ref[...], 1)
    prev_ref[...] = rw[rw.shape[0] - SUBLANES:]
    rw_ref[...] = rw + (rw_shift - rw) * mu_ref[...]

    logits = lbl_ref[...]
    e = jnp.exp(logits - jnp.max(logits, axis=0, keepdims=True))
    lb = e[0:1] / jnp.sum(e, axis=0, keepdims=True)

    st = [st_ref[h] for h in range(HG_HEADS)]
    for i in range(ROW_TILE // HG_TILE):
        rows = slice(i * HG_TILE, (i + 1) * HG_TILE)
        ohg_ref[rows, :], st = _hgrn2_rows(hg[rows], lb, hnw_ref[...], st)
    for h in range(HG_HEADS):
        st_ref[h] = st[h]


def _in_hgrn2(x2, norm_w, w_in_bf, lb_logits, hg_norm_w, rw_mu, batch, seq):
    m = x2.shape[0]
    nt = seq // ROW_TILE

    def rows(width):
        return pl.BlockSpec((ROW_TILE, width), lambda b, t: (b * nt + t, 0))

    def whole(shape):
        return pl.BlockSpec(shape, lambda b, t: (0, 0), pipeline_mode=pl.Buffered(1))

    return pl.pallas_call(
        _in_hgrn2_kernel,
        grid=(batch, nt),
        in_specs=[rows(D_MODEL), whole((1, D_MODEL)), whole((D_MODEL, IN_COLS)),
                  whole((2, HG_WIDTH)), whole((1, HG_WIDTH)), whole((1, RW_COLS))],
        out_specs=[rows(HG_WIDTH), rows(RW_COLS)],
        out_shape=[
            jax.ShapeDtypeStruct((m, HG_WIDTH), BF16),
            jax.ShapeDtypeStruct((m, RW_COLS), F32),
        ],
        scratch_shapes=[pltpu.VMEM((HG_HEADS, HG_HEAD_DIM, HG_HEAD_DIM), F32),
                        pltpu.VMEM((SUBLANES, RW_COLS), F32)],
        compiler_params=pltpu.CompilerParams(
            dimension_semantics=("arbitrary", "arbitrary"), vmem_limit_bytes=VMEM_LIMIT_BYTES),
        name="in_hgrn2",
    )(x2, norm_w, w_in_bf, lb_logits, hg_norm_w, rw_mu)


def _stack2(x):
    lane = _iota2(x.shape, 1)
    return jnp.concatenate([jnp.where(lane < RW_HEAD_DIM, x, 0.0),
                            jnp.where(lane >= RW_HEAD_DIM, x, 0.0)], axis=0)


def _rwkv7_kernel(x_ref, w0_ref, wwa_ref, a0_ref, g2_ref, kk_ref, ka_ref, rk_ref, lnw_ref, lnb_ref,
                  o_ref, z_ref):
    @pl.when(pl.program_id(1) == 0)
    def _():
        z_ref[...] = jnp.zeros_like(z_ref)

    w = RW_WIDTH
    lane = _iota2((LANES, LANES), 1)
    row = _iota2((LANES, LANES), 0)
    same_head = (row // RW_HEAD_DIM) == (lane // RW_HEAD_DIM)
    head_ones = jnp.where(same_head, 1.0, 0.0).astype(F32)

    def head_sums(t):
        return jnp.concatenate(
            [_dot_hilo_lhs(t[:, p * LANES:(p + 1) * LANES], head_ones) for p in range(RW_PAIRS)], axis=1)

    wrow = _iota2((CHUNK, LANES), 0)
    wcol = _iota2((CHUNK, LANES), 1) % CHUNK
    strict = wcol < wrow
    incl = wcol <= wrow
    eye_w = jnp.where(wcol == wrow, 1.0, 0.0).astype(F32)
    eye = lane == row
    pairs = range(RW_PAIRS)

    def gates(rows):
        r, k, v = x_ref[rows, :w], x_ref[rows, w:2 * w], x_ref[rows, 2 * w:3 * w]
        wa = x_ref[rows, 3 * w:3 * w + LANES]
        gd = x_ref[rows, 3 * w + LANES:]
        n = r.shape[0]

        wa = jnp.where(_iota2(wa.shape, 1) < RW_DECAY_LORA, jnp.tanh(wa), wa)
        lora = _dot(wa, wwa_ref[...])
        zw = -(w0_ref[...] + lora[:, :w])
        softplus = jnp.maximum(zw, 0.0) + jnp.log(1.0 + jnp.exp(-jnp.abs(zw)))
        logw = -jnp.exp(-softplus - 0.5)
        a = jax.nn.sigmoid(a0_ref[...] + lora[:, w:])
        g = _dot(jax.nn.sigmoid(gd), g2_ref[...])

        kk = k * kk_ref[...]
        kk = kk / jnp.maximum(jnp.sqrt(head_sums(kk * kk)), L2_EPS)
        k = k * (1.0 + (a - 1.0) * ka_ref[...])
        beta = kk * a
        bonus = head_sums(r * k * rk_ref[...]) * v

        cum = _dot_exact_rhs(_chunk_tril(n), logw)
        e_neg = jnp.exp(-cum)
        return dict(n=n, cum=cum, a_t=-kk * jnp.exp(cum - logw), r_t=r * jnp.exp(cum), b_t=beta * e_neg,
                    k_t=k * e_neg, beta=beta, k=k, v=v, bonus=bonus, g=g)

    def recur(d, z):
        chunks = range(d["n"] // CHUNK)
        probs = [(p, c) for p in pairs for c in chunks]

        def blk(t, p, c):
            return t[c * CHUNK:(c + 1) * CHUNK, p * LANES:(p + 1) * LANES]

        c_last = [blk(d["cum"], p, c)[CHUNK - 1:CHUNK] for p, c in probs]
        e_last = [jnp.exp(cl - blk(d["cum"], p, c)) for (p, c), cl in zip(probs, c_last)]
        a_c = [blk(d["a_t"], p, c) for p, c in probs]
        r_c = [blk(d["r_t"], p, c) for p, c in probs]
        b_h = [blk(d["beta"], p, c) * e for (p, c), e in zip(probs, e_last)]
        k_h = [blk(d["k"], p, c) * e for (p, c), e in zip(probs, e_last)]
        vc = [blk(d["v"], p, c) for p, c in probs]
        v_st = [_stack2(t) for t in vc]

        s4 = [_dot_nt(jnp.concatenate([aa, rr], axis=0),
                      jnp.concatenate([_stack2(blk(d["b_t"], p, c)), _stack2(blk(d["k_t"], p, c))], axis=0))
              for aa, rr, (p, c) in zip(a_c, r_c, probs)]
        l_ab = [jnp.where(strict, s[:CHUNK, :LANES], 0.0) for s in s4]
        l_ak = [jnp.where(strict, s[:CHUNK, LANES:], 0.0) for s in s4]
        a_rb = [jnp.where(incl, s[CHUNK:, :LANES], 0.0) for s in s4]
        a_rk = [jnp.where(incl, s[CHUNK:, LANES:], 0.0) for s in s4]

        tm = [l + eye_w for l in l_ab]
        lp = [_dot(l, _stack2(l)) for l in l_ab]
        w1 = [_dot(l, t) for l, t in zip(l_ak, v_st)]
        for _ in range(4):
            both = [_dot(jnp.concatenate([t, l], axis=0), _stack2(l)) for t, l in zip(tm, lp)]
            tm = [t + bo[:CHUNK] for t, bo in zip(tm, both)]
            lp = [bo[CHUNK:] for bo in both]
        tm = [t + _dot(t, _stack2(l)) for t, l in zip(tm, lp)]

        mp = [_dot(t, jnp.concatenate([_stack2(aa), _stack2(ww)], axis=1))
              for t, aa, ww in zip(tm, a_c, w1)]
        ry = [_dot(jnp.concatenate([lb, lk], axis=1),
                   jnp.concatenate([jnp.concatenate([_stack2(t[:, :LANES]), _stack2(t[:, LANES:])], axis=1),
                                    jnp.concatenate([jnp.zeros_like(vs), vs], axis=1)], axis=0))
              for lb, lk, t, vs in zip(a_rb, a_rk, mp, v_st)]
        rq = [rr + t[:, :LANES] for rr, t in zip(r_c, ry)]
        y0 = [t[:, LANES:] for t in ry]
        gu = [_dot_tn(jnp.concatenate([bh, kh], axis=0),
                      jnp.concatenate([t, jnp.concatenate([jnp.zeros_like(vv), vv], axis=1)], axis=0))
              for bh, kh, t, vv in zip(b_h, k_h, mp, vc)]
        g_mat = [jnp.where(same_head, t[:, :LANES], 0.0) + jnp.where(eye, jnp.exp(cl), 0.0)
                 for t, cl in zip(gu, c_last)]
        u_mat = [jnp.where(same_head, t[:, LANES:], 0.0) for t in gu]

        z = list(z)
        ys = [[None] * len(chunks) for _ in pairs]
        for c in chunks:
            for p in pairs:
                j = p * len(chunks) + c
                out = _dot(jnp.concatenate([g_mat[j], rq[j]], axis=0), z[p])
                ys[p][c] = out[LANES:] + y0[j]
                z[p] = out[:LANES] + u_mat[j]
        return jnp.concatenate([jnp.concatenate(ys[p], axis=0) for p in pairs], axis=1), z

    def finish(y, d):
        inv_n = 1.0 / RW_HEAD_DIM
        mean = head_sums(y) * inv_n
        yc = y - mean
        var = head_sums(yc * yc) * inv_n
        y = yc * lax.rsqrt(var + RW_GN_EPS) * lnw_ref[...] + lnb_ref[...]
        return _bf((y + d["bonus"]) * d["g"])

    z = [z_ref[p] for p in pairs]
    pending = None
    for i in range(x_ref.shape[0] // RW_BLOCK):
        rows = slice(i * RW_BLOCK, (i + 1) * RW_BLOCK)
        d = gates(rows)
        y, z = recur(d, z)
        if pending is not None:
            o_ref[pending[0], :] = finish(*pending[1:])
        pending = (rows, y, d)
    o_ref[pending[0], :] = finish(*pending[1:])
    for p in pairs:
        z_ref[p] = z[p]


def _rwkv7(rw_p, w0, wwa_bf, a0, g2_bf, k_k, k_a, r_k, ln_w, ln_b, batch, seq):
    m = rw_p.shape[0]
    nt = seq // SCAN_TILE

    def vec(width):
        return pl.BlockSpec((1, width), lambda b, t: (0, 0))

    return pl.pallas_call(
        _rwkv7_kernel,
        grid=(batch, nt),
        in_specs=[
            pl.BlockSpec((SCAN_TILE, RW_COLS), lambda b, t: (b * nt + t, 0)),
            vec(RW_WIDTH),
            pl.BlockSpec((LANES, 2 * RW_WIDTH), lambda b, t: (0, 0)),
            vec(RW_WIDTH),
            pl.BlockSpec((RW_GATE_LORA, RW_WIDTH), lambda b, t: (0, 0)),
            vec(RW_WIDTH), vec(RW_WIDTH), vec(RW_WIDTH), vec(RW_WIDTH), vec(RW_WIDTH),
        ],
        out_specs=pl.BlockSpec((SCAN_TILE, RW_WIDTH), lambda b, t: (b * nt + t, 0)),
        out_shape=jax.ShapeDtypeStruct((m, RW_WIDTH), BF16),
        scratch_shapes=[pltpu.VMEM((RW_PAIRS, LANES, LANES), F32)],
        compiler_params=pltpu.CompilerParams(
            dimension_semantics=("arbitrary", "arbitrary"), vmem_limit_bytes=VMEM_LIMIT_BYTES),
        name="rwkv7",
    )(rw_p, w0, wwa_bf, a0, g2_bf, k_k, k_a, r_k, ln_w, ln_b)


def _mix_ffn_kernel(x_ref, ohg_ref, orw_ref, wo_ref, n2_ref, wu_ref, cw_ref, cb_ref, wd_ref, nf_ref,
                    o_ref, prev_ref, u0_ref, u1_ref, act_ref):
    u_refs = (u0_ref, u1_ref)
    hist = SUBLANES

    @pl.when(pl.program_id(1) == 0)
    def _():
        prev_ref[...] = jnp.zeros_like(prev_ref)

    x1 = (x_ref[...]
          + jnp.dot(ohg_ref[...], wo_ref[:HG_WIDTH], preferred_element_type=F32)
          + jnp.dot(orw_ref[...], wo_ref[HG_WIDTH:], preferred_element_type=F32))
    h = _bf(_rmsnorm(x1, n2_ref[...]))
    n = h.shape[0]

    ct = FFN_COL_TILE
    ntile = D_FF // ct

    def halves(j):
        return ((slice(0, ct), slice(j * ct, (j + 1) * ct)),
                (slice(ct, 2 * ct), slice(D_FF + j * ct, D_FF + (j + 1) * ct)))

    def up(j):
        u = u_refs[j % 2]
        for dst, cols in halves(j):
            u[0:hist, dst] = prev_ref[:, cols]
            u[hist:, dst] = jnp.dot(h, wu_ref[:, cols], preferred_element_type=F32)
            prev_ref[:, cols] = u[n:n + hist, dst]

    def conv_gate(j):
        u = u_refs[j % 2]
        c = []
        for dst, cols in halves(j):
            cw = cw_ref[:, cols]
            c.append(cw[2:3] * u[hist:n + hist, dst] + cw[1:2] * u[hist - 1:n + hist - 1, dst]
                     + cw[0:1] * u[hist - 2:n + hist - 2, dst] + cb_ref[:, cols])
        act_ref[:, j * ct:(j + 1) * ct] = _bf(_silu(c[0]) * c[1])

    up(0)
    for j in range(ntile):
        if j + 1 < ntile:
            up(j + 1)
        conv_gate(j)
    y = x1 + jnp.dot(act_ref[...], wd_ref[...], preferred_element_type=F32)
    o_ref[...] = _rmsnorm(y, nf_ref[...])


def _mix_ffn(x2, o_hg, o_rw, w_out_bf, norm2_w, w_up_bf, conv_w, conv_b, w_down_bf, final_w, batch, seq):
    m = x2.shape[0]
    nt = seq // ROW_TILE

    def rows(width):
        return pl.BlockSpec((ROW_TILE, width), lambda b, t: (b * nt + t, 0))

    def whole(shape):
        return pl.BlockSpec(shape, lambda b, t: (0, 0), pipeline_mode=pl.Buffered(1))

    return pl.pallas_call(
        _mix_ffn_kernel,
        grid=(batch, nt),
        in_specs=[
            rows(D_MODEL), rows(HG_WIDTH), rows(RW_WIDTH),
            whole((D_MODEL, D_MODEL)), whole((1, D_MODEL)),
            whole((D_MODEL, 2 * D_FF)), whole((3, 2 * D_FF)), whole((1, 2 * D_FF)),
            whole((D_FF, D_MODEL)), whole((1, D_MODEL)),
        ],
        out_specs=rows(D_MODEL),
        out_shape=jax.ShapeDtypeStruct((m, D_MODEL), F32),
        scratch_shapes=[pltpu.VMEM((SUBLANES, 2 * D_FF), F32),
                        pltpu.VMEM((SUBLANES + ROW_TILE, 2 * FFN_COL_TILE), F32),
                        pltpu.VMEM((SUBLANES + ROW_TILE, 2 * FFN_COL_TILE), F32),
                        pltpu.VMEM((ROW_TILE, D_FF), BF16)],
        compiler_params=pltpu.CompilerParams(
            dimension_semantics=("arbitrary", "arbitrary"), vmem_limit_bytes=VMEM_LIMIT_BYTES),
        name="mix_ffn",
    )(x2, o_hg, o_rw, w_out_bf, norm2_w, w_up_bf, conv_w, conv_b, w_down_bf, final_w)


def kernel(x, norm1_w, w_in, hg_lb_logits, hg_norm_w, rw_shift_mu, rw_w0, rw_w2, rw_a0, rw_a2, rw_g2,
           rw_k_k, rw_k_a, rw_r_k, rw_ln_w, rw_ln_b, w_out, norm2_w, w_up, conv_w, conv_b, w_down,
           final_norm_w):
    batch, seq, _ = x.shape
    assert seq % SCAN_TILE == 0 and seq % ROW_TILE == 0 and norm1_w.shape[0] == 1
    assert CHUNK == RW_HEAD_DIM and 2 * RW_HEAD_DIM == LANES
    x2 = x.reshape(batch * seq, D_MODEL)

    zeros = jnp.zeros_like(rw_w2[0])
    wwa = _bf(jnp.concatenate([jnp.concatenate([rw_w2[0], zeros], axis=1),
                               jnp.concatenate([zeros, rw_a2[0]], axis=1)], axis=0))

    o_hg, rw_p = _in_hgrn2(x2, norm1_w, _bf(w_in[0]), hg_lb_logits, hg_norm_w, rw_shift_mu, batch, seq)
    o_rw = _rwkv7(rw_p, rw_w0, wwa, rw_a0, _bf(rw_g2[0]), rw_k_k, rw_k_a, rw_r_k, rw_ln_w, rw_ln_b, batch, seq)
    out = _mix_ffn(x2, o_hg, o_rw, _bf(w_out[0]), norm2_w, _bf(w_up[0]), conv_w[0], conv_b, _bf(w_down[0]),
                   final_norm_w.reshape(1, D_MODEL), batch, seq)
    return out.reshape(batch, seq, D_MODEL)
```

```python
import jax
import jax.numpy as jnp
from jax import lax
from jax.experimental import pallas as pl
from jax.experimental.pallas import tpu as pltpu

F32 = jnp.float32
BF16 = jnp.bfloat16

D_MODEL = 1024
HG_WIDTH = 512
HG_HEAD_DIM = 128
HG_HEADS = HG_WIDTH // HG_HEAD_DIM
RW_WIDTH = 512
RW_HEAD_DIM = 64
RW_DECAY_LORA = 64
RW_AAA_LORA = 64
RW_GATE_LORA = 128
RW_COLS = 3 * RW_WIDTH + RW_DECAY_LORA + RW_AAA_LORA + RW_GATE_LORA
HG_COLS = 4 * HG_WIDTH
IN_COLS = HG_COLS + RW_COLS
D_FF = 2816
NORM_EPS = 1e-6
RW_GN_EPS = 64e-5
L2_EPS = 1e-12

LANES = 128
SUBLANES = 8
CHUNK = 64
RW_PAIRS = RW_WIDTH // LANES
VMEM_LIMIT_BYTES = 56 * 1024 * 1024

ROW_TILE = 512
SCAN_TILE = 512
RW_BLOCK = 256
HG_TILE = 256
FFN_COL_TILE = 256


def _bf(x):
    return x.astype(BF16)


def _dot(a, b):
    return jnp.dot(_bf(a), _bf(b), preferred_element_type=F32)


def _dot_nt(a, b):
    return lax.dot_general(_bf(a), _bf(b), (((1,), (1,)), ((), ())), preferred_element_type=F32)


def _dot_tn(a, b):
    return jnp.dot(_bf(a.T), _bf(b), preferred_element_type=F32)


def _split3(x):
    h1 = _bf(x)
    r1 = x - h1.astype(F32)
    h2 = _bf(r1)
    r2 = r1 - h2.astype(F32)
    return h1, h2, _bf(r2)


def _dot_exact_rhs(m01, x):
    w = x.shape[1]
    out = jnp.dot(_bf(m01), jnp.concatenate(_split3(x), axis=1), preferred_element_type=F32)
    return out[:, :w] + out[:, w:2 * w] + out[:, 2 * w:]


def _dot_hilo_lhs(x, m01):
    n = x.shape[0]
    hi = _bf(x)
    lo = _bf(x - hi.astype(F32))
    out = jnp.dot(jnp.concatenate([hi, lo], axis=0), _bf(m01), preferred_element_type=F32)
    return out[:n] + out[n:]


def _iota2(shape, dim):
    return lax.broadcasted_iota(jnp.int32, shape, dim)


def _chunk_tril(n):
    r = _iota2((n, n), 0)
    c = _iota2((n, n), 1)
    return jnp.where((r // CHUNK == c // CHUNK) & (c <= r), 1.0, 0.0).astype(F32)


def _shift_rows(x, prev_rows, k):
    rolled = pltpu.roll(x, k, axis=0)
    prev = pltpu.roll(prev_rows, k, axis=0)
    head = jnp.where(_iota2(prev.shape, 0) < k, prev, rolled[:SUBLANES])
    return jnp.concatenate([head, rolled[SUBLANES:]], axis=0)


def _silu(x):
    return x * jax.nn.sigmoid(x)


def _rmsnorm(x, w):
    return x * lax.rsqrt(jnp.mean(x * x, axis=-1, keepdims=True) + NORM_EPS) * w


def _hgrn2_rows(x, lb, norm_w, st):
    w = HG_WIDTH
    f = lb + (1.0 - lb) * jax.nn.sigmoid(x[:, w:2 * w])
    kk = 1.0 - f
    q = _silu(x[:, :w]) * (HG_HEAD_DIM ** -0.5)
    v = x[:, 2 * w:3 * w]
    n = q.shape[0]
    b = _dot_exact_rhs(_chunk_tril(n), jnp.log2(f))

    heads = range(HG_HEADS)
    chunks = range(n // CHUNK)
    probs = [(h, c) for h in heads for c in chunks]

    def blk(x, h, c):
        return x[c * CHUNK:(c + 1) * CHUNK, h * LANES:(h + 1) * LANES]

    def bcast_rows(x, period, offset):
        return jnp.concatenate(
            [jnp.broadcast_to(x[g * period + offset:g * period + offset + 1], (period, x.shape[1]))
             for g in range(x.shape[0] // period)], axis=0)

    t_idx = _iota2(b.shape, 0)
    srow = _iota2((CHUNK, CHUNK), 0)
    scol = _iota2((CHUNK, CHUNK), 1)
    q_bf, k_bf = _bf(q), _bf(kk)
    q_lv, k_lv, sel = [_bf(q * f)], [k_bf], []
    size = 2
    while size <= CHUNK:
        half = size // 2
        if size == 4:
            pos = t_idx % size
            m = jnp.where(pos == 0, pltpu.roll(b, n - 1, axis=0),
                          jnp.where(pos == 1, b,
                                    jnp.where(pos == 2, pltpu.roll(b, 1, axis=0), pltpu.roll(b, 2, axis=0))))
        elif size > 4:
            m = bcast_rows(b, size, half - 1)
        if size > 2:
            ez = _bf(jnp.exp2(-jnp.abs(b - m)))
            q_lv.append(q_bf * ez)
            k_lv.append(k_bf * ez)
        sel.append((srow // size == scol // size) & (srow % size >= half) & (scol % size < half))
        size *= 2
    qe_all = q * jnp.exp2(b)
    kl_all = kk * jnp.exp2(bcast_rows(b, CHUNK, CHUNK - 1) - b)

    b_last = [blk(b, h, c)[CHUNK - 1:CHUNK] for h, c in probs]
    qe = [blk(qe_all, h, c) for h, c in probs]
    vc = [blk(v, h, c) for h, c in probs]

    s_lv = [[_dot_nt(blk(ql, h, c), blk(kl, h, c)) for h, c in probs] for ql, kl in zip(q_lv, k_lv)]
    a = []
    for j in range(len(probs)):
        acc = jnp.zeros((CHUNK, CHUNK), F32)
        for lv in range(len(sel)):
            acc = jnp.where(sel[lv], s_lv[lv][j], acc)
        a.append(acc)
    u = [_dot_tn(x, blk(kl_all, h, c)) for x, (h, c) in zip(vc, probs)]
    qk = q * kk
    o_intra = [_dot(x, y) + jnp.sum(blk(qk, h, c), axis=-1, keepdims=True) * y
               for x, y, (h, c) in zip(a, vc, probs)]

    st = list(st)
    outs = [[None] * len(chunks) for _ in heads]
    for c in chunks:
        for h in heads:
            j = h * len(chunks) + c
            outs[h][c] = o_intra[j] + _dot_nt(qe[j], st[h])
            st[h] = st[h] * jnp.exp2(b_last[j]) + u[j]

    cols = []
    for h in heads:
        o = jnp.concatenate(outs[h], axis=0)
        cols.append(o * lax.rsqrt(jnp.mean(o * o, axis=-1, keepdims=True) + NORM_EPS))
    o = jnp.concatenate(cols, axis=1)
    return _bf(o * norm_w * _silu(x[:, 3 * w:])), st


def _in_hgrn2_kernel(x_ref, nw_ref, w_ref, lbl_ref, hnw_ref, mu_ref, ohg_ref, rw_ref, st_ref, prev_ref):
    @pl.when(pl.program_id(1) == 0)
    def _():
        st_ref[...] = jnp.zeros_like(st_ref)
        prev_ref[...] = jnp.zeros_like(prev_ref)

    hb = _bf(_rmsnorm(x_ref[...], nw_ref[...]))
    rw = jnp.dot(hb, w_ref[:, HG_COLS:], preferred_element_type=F32)
    hg = jnp.dot(hb, w_ref[:, :HG_COLS], preferred_element_type=F32)
    rw_shift = _shift_rows(rw, prev_ref[...], 1)
    prev_ref[...] = rw[rw.shape[0] - SUBLANES:]
    rw_ref[...] = rw + (rw_shift - rw) * mu_ref[...]

    logits = lbl_ref[...]
    e = jnp.exp(logits - jnp.max(logits, axis=0, keepdims=True))
    lb = e[0:1] / jnp.sum(e, axis=0, keepdims=True)

    st = [st_ref[h] for h in range(HG_HEADS)]
    for i in range(ROW_TILE // HG_TILE):
        rows = slice(i * HG_TILE, (i + 1) * HG_TILE)
        ohg_ref[rows, :], st = _hgrn2_rows(hg[rows], lb, hnw_ref[...], st)
    for h in range(HG_HEADS):
        st_ref[h] = st[h]


def _in_hgrn2(x2, norm_w, w_in_bf, lb_logits, hg_norm_w, rw_mu, batch, seq):
    m = x2.shape[0]
    nt = seq // ROW_TILE

    def rows(width):
        return pl.BlockSpec((ROW_TILE, width), lambda b, t: (b * nt + t, 0))

    def whole(shape):
        return pl.BlockSpec(shape, lambda b, t: (0, 0), pipeline_mode=pl.Buffered(1))

    return pl.pallas_call(
        _in_hgrn2_kernel,
        grid=(batch, nt),
        in_specs=[rows(D_MODEL), whole((1, D_MODEL)), whole((D_MODEL, IN_COLS)),
                  whole((2, HG_WIDTH)), whole((1, HG_WIDTH)), whole((1, RW_COLS))],
        out_specs=[rows(HG_WIDTH), rows(RW_COLS)],
        out_shape=[
            jax.ShapeDtypeStruct((m, HG_WIDTH), BF16),
            jax.ShapeDtypeStruct((m, RW_COLS), F32),
        ],
        scratch_shapes=[pltpu.VMEM((HG_HEADS, HG_HEAD_DIM, HG_HEAD_DIM), F32),
                        pltpu.VMEM((SUBLANES, RW_COLS), F32)],
        compiler_params=pltpu.CompilerParams(
            dimension_semantics=("arbitrary", "arbitrary"), vmem_limit_bytes=VMEM_LIMIT_BYTES),
        name="in_hgrn2",
    )(x2, norm_w, w_in_bf, lb_logits, hg_norm_w, rw_mu)


def _stack2(x):
    lane = _iota2(x.shape, 1)
    return jnp.concatenate([jnp.where(lane < RW_HEAD_DIM, x, 0.0),
                            jnp.where(lane >= RW_HEAD_DIM, x, 0.0)], axis=0)


def _rwkv7_kernel(x_ref, w0_ref, wwa_ref, a0_ref, g2_ref, kk_ref, ka_ref, rk_ref, lnw_ref, lnb_ref,
                  o_ref, z_ref):
    @pl.when(pl.program_id(1) == 0)
    def _():
        z_ref[...] = jnp.zeros_like(z_ref)

    w = RW_WIDTH
    lane = _iota2((LANES, LANES), 1)
    row = _iota2((LANES, LANES), 0)
    same_head = (row // RW_HEAD_DIM) == (lane // RW_HEAD_DIM)
    head_ones = jnp.where(same_head, 1.0, 0.0).astype(F32)

    def head_sums(t, dot=_dot):
        return jnp.concatenate(
            [dot(t[:, p * LANES:(p + 1) * LANES], head_ones) for p in range(RW_PAIRS)], axis=1)

    wrow = _iota2((CHUNK, LANES), 0)
    wcol = _iota2((CHUNK, LANES), 1) % CHUNK
    strict = wcol < wrow
    incl = wcol <= wrow
    eye_w = jnp.where(wcol == wrow, 1.0, 0.0).astype(F32)
    eye = lane == row
    pairs = range(RW_PAIRS)

    def gates(rows):
        r, k, v = x_ref[rows, :w], x_ref[rows, w:2 * w], x_ref[rows, 2 * w:3 * w]
        wa = x_ref[rows, 3 * w:3 * w + LANES]
        gd = x_ref[rows, 3 * w + LANES:]
        n = r.shape[0]

        wa = jnp.where(_iota2(wa.shape, 1) < RW_DECAY_LORA, jnp.tanh(wa), wa)
        lora = _dot(wa, wwa_ref[...])
        zw = -(w0_ref[...] + lora[:, :w])
        softplus = jnp.maximum(zw, 0.0) + jnp.log(1.0 + jnp.exp(-jnp.abs(zw)))
        logw = -jnp.exp(-softplus - 0.5)
        a = jax.nn.sigmoid(a0_ref[...] + lora[:, w:])
        g = _dot(jax.nn.sigmoid(gd), g2_ref[...])

        kk = k * kk_ref[...]
        kk = kk / jnp.maximum(jnp.sqrt(head_sums(kk * kk)), L2_EPS)
        k = k * (1.0 + (a - 1.0) * ka_ref[...])
        beta = kk * a
        bonus = head_sums(r * k * rk_ref[...]) * v

        cum = _dot_exact_rhs(_chunk_tril(n), logw)
        e_neg = jnp.exp(-cum)
        return dict(n=n, cum=cum, a_t=-kk * jnp.exp(cum - logw), r_t=r * jnp.exp(cum), b_t=beta * e_neg,
                    k_t=k * e_neg, beta=beta, k=k, v=v, bonus=bonus, g=g)

    def recur(d, z):
        chunks = range(d["n"] // CHUNK)
        probs = [(p, c) for p in pairs for c in chunks]

        def blk(t, p, c):
            return t[c * CHUNK:(c + 1) * CHUNK, p * LANES:(p + 1) * LANES]

        c_last = [blk(d["cum"], p, c)[CHUNK - 1:CHUNK] for p, c in probs]
        e_last = [jnp.exp(cl - blk(d["cum"], p, c)) for (p, c), cl in zip(probs, c_last)]
        a_c = [blk(d["a_t"], p, c) for p, c in probs]
        r_c = [blk(d["r_t"], p, c) for p, c in probs]
        b_h = [blk(d["beta"], p, c) * e for (p, c), e in zip(probs, e_last)]
        k_h = [blk(d["k"], p, c) * e for (p, c), e in zip(probs, e_last)]
        vc = [blk(d["v"], p, c) for p, c in probs]
        v_st = [_stack2(t) for t in vc]

        s4 = [_dot_nt(jnp.concatenate([aa, rr], axis=0),
                      jnp.concatenate([_stack2(blk(d["b_t"], p, c)), _stack2(blk(d["k_t"], p, c))], axis=0))
              for aa, rr, (p, c) in zip(a_c, r_c, probs)]
        l_ab = [jnp.where(strict, s[:CHUNK, :LANES], 0.0) for s in s4]
        l_ak = [jnp.where(strict, s[:CHUNK, LANES:], 0.0) for s in s4]
        a_rb = [jnp.where(incl, s[CHUNK:, :LANES], 0.0) for s in s4]
        a_rk = [jnp.where(incl, s[CHUNK:, LANES:], 0.0) for s in s4]

        tm = [l + eye_w for l in l_ab]
        lp = [_dot(l, _stack2(l)) for l in l_ab]
        w1 = [_dot(l, t) for l, t in zip(l_ak, v_st)]
        for _ in range(4):
            both = [_dot(jnp.concatenate([t, l], axis=0), _stack2(l)) for t, l in zip(tm, lp)]
            tm = [t + bo[:CHUNK] for t, bo in zip(tm, both)]
            lp = [bo[CHUNK:] for bo in both]
        tm = [t + _dot(t, _stack2(l)) for t, l in zip(tm, lp)]

        mp = [_dot(t, jnp.concatenate([_stack2(aa), _stack2(ww)], axis=1))
              for t, aa, ww in zip(tm, a_c, w1)]
        ry = [_dot(jnp.concatenate([lb, lk], axis=1),
                   jnp.concatenate([jnp.concatenate([_stack2(t[:, :LANES]), _stack2(t[:, LANES:])], axis=1),
                                    jnp.concatenate([jnp.zeros_like(vs), vs], axis=1)], axis=0))
              for lb, lk, t, vs in zip(a_rb, a_rk, mp, v_st)]
        rq = [rr + t[:, :LANES] for rr, t in zip(r_c, ry)]
        y0 = [t[:, LANES:] for t in ry]
        gu = [_dot_tn(jnp.concatenate([bh, kh], axis=0),
                      jnp.concatenate([t, jnp.concatenate([jnp.zeros_like(vv), vv], axis=1)], axis=0))
              for bh, kh, t, vv in zip(b_h, k_h, mp, vc)]
        g_mat = [jnp.where(same_head, t[:, :LANES], 0.0) + jnp.where(eye, jnp.exp(cl), 0.0)
                 for t, cl in zip(gu, c_last)]
        u_mat = [jnp.where(same_head, t[:, LANES:], 0.0) for t in gu]

        z = list(z)
        ys = [[None] * len(chunks) for _ in pairs]
        for c in chunks:
            for p in pairs:
                j = p * len(chunks) + c
                out = _dot(jnp.concatenate([g_mat[j], rq[j]], axis=0), z[p])
                ys[p][c] = out[LANES:] + y0[j]
                z[p] = out[:LANES] + u_mat[j]
        return jnp.concatenate([jnp.concatenate(ys[p], axis=0) for p in pairs], axis=1), z

    def finish(y, d):
        inv_n = 1.0 / RW_HEAD_DIM
        mean = head_sums(y, _dot_hilo_lhs) * inv_n
        yc = y - mean
        var = head_sums(yc * yc) * inv_n
        y = yc * lax.rsqrt(var + RW_GN_EPS) * lnw_ref[...] + lnb_ref[...]
        return _bf((y + d["bonus"]) * d["g"])

    z = [z_ref[p] for p in pairs]
    pending = None
    for i in range(x_ref.shape[0] // RW_BLOCK):
        rows = slice(i * RW_BLOCK, (i + 1) * RW_BLOCK)
        d = gates(rows)
        y, z = recur(d, z)
        if pending is not None:
            o_ref[pending[0], :] = finish(*pending[1:])
        pending = (rows, y, d)
    o_ref[pending[0], :] = finish(*pending[1:])
    for p in pairs:
        z_ref[p] = z[p]


def _rwkv7(rw_p, w0, wwa_bf, a0, g2_bf, k_k, k_a, r_k, ln_w, ln_b, batch, seq):
    m = rw_p.shape[0]
    nt = seq // SCAN_TILE

    def vec(width):
        return pl.BlockSpec((1, width), lambda b, t: (0, 0))

    return pl.pallas_call(
        _rwkv7_kernel,
        grid=(batch, nt),
        in_specs=[
            pl.BlockSpec((SCAN_TILE, RW_COLS), lambda b, t: (b * nt + t, 0)),
            vec(RW_WIDTH),
            pl.BlockSpec((LANES, 2 * RW_WIDTH), lambda b, t: (0, 0)),
            vec(RW_WIDTH),
            pl.BlockSpec((RW_GATE_LORA, RW_WIDTH), lambda b, t: (0, 0)),
            vec(RW_WIDTH), vec(RW_WIDTH), vec(RW_WIDTH), vec(RW_WIDTH), vec(RW_WIDTH),
        ],
        out_specs=pl.BlockSpec((SCAN_TILE, RW_WIDTH), lambda b, t: (b * nt + t, 0)),
        out_shape=jax.ShapeDtypeStruct((m, RW_WIDTH), BF16),
        scratch_shapes=[pltpu.VMEM((RW_PAIRS, LANES, LANES), F32)],
        compiler_params=pltpu.CompilerParams(
            dimension_semantics=("arbitrary", "arbitrary"), vmem_limit_bytes=VMEM_LIMIT_BYTES),
        name="rwkv7",
    )(rw_p, w0, wwa_bf, a0, g2_bf, k_k, k_a, r_k, ln_w, ln_b)


def _mix_ffn_kernel(x_ref, ohg_ref, orw_ref, wo_ref, n2_ref, wu_ref, cw_ref, cb_ref, wd_ref, nf_ref,
                    o_ref, prev_ref, u0_ref, u1_ref, act_ref):
    u_refs = (u0_ref, u1_ref)
    hist = SUBLANES

    @pl.when(pl.program_id(1) == 0)
    def _():
        prev_ref[...] = jnp.zeros_like(prev_ref)

    x1 = (x_ref[...]
          + jnp.dot(ohg_ref[...], wo_ref[:HG_WIDTH], preferred_element_type=F32)
          + jnp.dot(orw_ref[...], wo_ref[HG_WIDTH:], preferred_element_type=F32))
    h = _bf(_rmsnorm(x1, n2_ref[...]))
    n = h.shape[0]

    ct = FFN_COL_TILE
    ntile = D_FF // ct

    def halves(j):
        return ((slice(0, ct), slice(j * ct, (j + 1) * ct)),
                (slice(ct, 2 * ct), slice(D_FF + j * ct, D_FF + (j + 1) * ct)))

    def up(j):
        u = u_refs[j % 2]
        for dst, cols in halves(j):
            u[0:hist, dst] = prev_ref[:, cols]
            u[hist:, dst] = jnp.dot(h, wu_ref[:, cols], preferred_element_type=F32)
            prev_ref[:, cols] = u[n:n + hist, dst]

    def conv_gate(j):
        u = u_refs[j % 2]
        c = []
        for dst, cols in halves(j):
            cw = cw_ref[:, cols]
            c.append(cw[2:3] * u[hist:n + hist, dst] + cw[1:2] * u[hist - 1:n + hist - 1, dst]
                     + cw[0:1] * u[hist - 2:n + hist - 2, dst] + cb_ref[:, cols])
        act_ref[:, j * ct:(j + 1) * ct] = _bf(_silu(c[0]) * c[1])

    up(0)
    for j in range(ntile):
        if j + 1 < ntile:
            up(j + 1)
        conv_gate(j)
    y = x1 + jnp.dot(act_ref[...], wd_ref[...], preferred_element_type=F32)
    o_ref[...] = _rmsnorm(y, nf_ref[...])


def _mix_ffn(x2, o_hg, o_rw, w_out_bf, norm2_w, w_up_bf, conv_w, conv_b, w_down_bf, final_w, batch, seq):
    m = x2.shape[0]
    nt = seq // ROW_TILE

    def rows(width):
        return pl.BlockSpec((ROW_TILE, width), lambda b, t: (b * nt + t, 0))

    def whole(shape):
        return pl.BlockSpec(shape, lambda b, t: (0, 0), pipeline_mode=pl.Buffered(1))

    return pl.pallas_call(
        _mix_ffn_kernel,
        grid=(batch, nt),
        in_specs=[
            rows(D_MODEL), rows(HG_WIDTH), rows(RW_WIDTH),
            whole((D_MODEL, D_MODEL)), whole((1, D_MODEL)),
            whole((D_MODEL, 2 * D_FF)), whole((3, 2 * D_FF)), whole((1, 2 * D_FF)),
            whole((D_FF, D_MODEL)), whole((1, D_MODEL)),
        ],
        out_specs=rows(D_MODEL),
        out_shape=jax.ShapeDtypeStruct((m, D_MODEL), F32),
        scratch_shapes=[pltpu.VMEM((SUBLANES, 2 * D_FF), F32),
                        pltpu.VMEM((SUBLANES + ROW_TILE, 2 * FFN_COL_TILE), F32),
                        pltpu.VMEM((SUBLANES + ROW_TILE, 2 * FFN_COL_TILE), F32),
                        pltpu.VMEM((ROW_TILE, D_FF), BF16)],
        compiler_params=pltpu.CompilerParams(
            dimension_semantics=("arbitrary", "arbitrary"), vmem_limit_bytes=VMEM_LIMIT_BYTES),
        name="mix_ffn",
    )(x2, o_hg, o_rw, w_out_bf, norm2_w, w_up_bf, conv_w, conv_b, w_down_bf, final_w)


def kernel(x, norm1_w, w_in, hg_lb_logits, hg_norm_w, rw_shift_mu, rw_w0, rw_w2, rw_a0, rw_a2, rw_g2,
           rw_k_k, rw_k_a, rw_r_k, rw_ln_w, rw_ln_b, w_out, norm2_w, w_up, conv_w, conv_b, w_down,
           final_norm_w):
    batch, seq, _ = x.shape
    assert seq % SCAN_TILE == 0 and seq % ROW_TILE == 0 and norm1_w.shape[0] == 1
    assert CHUNK == RW_HEAD_DIM and 2 * RW_HEAD_DIM == LANES
    x2 = x.reshape(batch * seq, D_MODEL)

    zeros = jnp.zeros_like(rw_w2[0])
    wwa = _bf(jnp.concatenate([jnp.concatenate([rw_w2[0], zeros], axis=1),
                               jnp.concatenate([zeros, rw_a2[0]], axis=1)], axis=0))

    o_hg, rw_p = _in_hgrn2(x2, norm1_w, _bf(w_in[0]), hg_lb_logits, hg_norm_w, rw_shift_mu, batch, seq)
    o_rw = _rwkv7(rw_p, rw_w0, wwa, rw_a0, _bf(rw_g2[0]), rw_k_k, rw_k_a, rw_r_k, rw_ln_w, rw_ln_b, batch, seq)
    out = _mix_ffn(x2, o_hg, o_rw, _bf(w_out[0]), norm2_w, _bf(w_up[0]), conv_w[0], conv_b, _bf(w_down[0]),
                   final_norm_w.reshape(1, D_MODEL), batch, seq)
    return out.reshape(batch, seq, D_MODEL)
```

```python
import jax
import jax.numpy as jnp
from jax import lax
from jax.experimental import pallas as pl
from jax.experimental.pallas import tpu as pltpu

F32 = jnp.float32
BF16 = jnp.bfloat16

D_MODEL = 1024
HG_WIDTH = 512
HG_HEAD_DIM = 128
HG_HEADS = HG_WIDTH // HG_HEAD_DIM
RW_WIDTH = 512
RW_HEAD_DIM = 64
RW_DECAY_LORA = 64
RW_AAA_LORA = 64
RW_GATE_LORA = 128
RW_COLS = 3 * RW_WIDTH + RW_DECAY_LORA + RW_AAA_LORA + RW_GATE_LORA
HG_COLS = 4 * HG_WIDTH
IN_COLS = HG_COLS + RW_COLS
D_FF = 2816
NORM_EPS = 1e-6
RW_GN_EPS = 64e-5
L2_EPS = 1e-12
RW_DECAY_SCALE = 0.6065306597126334

LANES = 128
SUBLANES = 8
CHUNK = 64
RW_PAIRS = RW_WIDTH // LANES
VMEM_LIMIT_BYTES = 56 * 1024 * 1024

ROW_TILE = 512
SCAN_TILE = 512
RW_BLOCK = 256
HG_TILE = 256
FFN_COL_TILE = 256


def _bf(x):
    return x.astype(BF16)


def _dot(a, b):
    return jnp.dot(_bf(a), _bf(b), preferred_element_type=F32)


def _dot_nt(a, b):
    return lax.dot_general(_bf(a), _bf(b), (((1,), (1,)), ((), ())), preferred_element_type=F32)


def _dot_tn(a, b):
    return jnp.dot(_bf(a.T), _bf(b), preferred_element_type=F32)


def _split3(x):
    h1 = _bf(x)
    r1 = x - h1.astype(F32)
    h2 = _bf(r1)
    r2 = r1 - h2.astype(F32)
    return h1, h2, _bf(r2)


def _dot_exact_rhs(m01, x):
    w = x.shape[1]
    out = jnp.dot(_bf(m01), jnp.concatenate(_split3(x), axis=1), preferred_element_type=F32)
    return out[:, :w] + out[:, w:2 * w] + out[:, 2 * w:]


def _dot_hilo_lhs(x, m01):
    n = x.shape[0]
    hi = _bf(x)
    lo = _bf(x - hi.astype(F32))
    out = jnp.dot(jnp.concatenate([hi, lo], axis=0), _bf(m01), preferred_element_type=F32)
    return out[:n] + out[n:]


def _iota2(shape, dim):
    return lax.broadcasted_iota(jnp.int32, shape, dim)


def _chunk_tril(n):
    r = _iota2((n, n), 0)
    c = _iota2((n, n), 1)
    return jnp.where((r // CHUNK == c // CHUNK) & (c <= r), 1.0, 0.0).astype(F32)


def _shift_rows(x, prev_rows, k):
    rolled = pltpu.roll(x, k, axis=0)
    prev = pltpu.roll(prev_rows, k, axis=0)
    head = jnp.where(_iota2(prev.shape, 0) < k, prev, rolled[:SUBLANES])
    return jnp.concatenate([head, rolled[SUBLANES:]], axis=0)


def _silu(x):
    return x * jax.nn.sigmoid(x)


def _rmsnorm(x, w):
    return x * lax.rsqrt(jnp.mean(x * x, axis=-1, keepdims=True) + NORM_EPS) * w


def _hgrn2_rows(x, lb, norm_w, st):
    w = HG_WIDTH
    f = lb + (1.0 - lb) * jax.nn.sigmoid(x[:, w:2 * w])
    kk = 1.0 - f
    q = _silu(x[:, :w]) * (HG_HEAD_DIM ** -0.5)
    v = x[:, 2 * w:3 * w]
    n = q.shape[0]
    b = _dot_exact_rhs(_chunk_tril(n), jnp.log2(f))

    heads = range(HG_HEADS)
    chunks = range(n // CHUNK)
    probs = [(h, c) for h in heads for c in chunks]

    def blk(x, h, c):
        return x[c * CHUNK:(c + 1) * CHUNK, h * LANES:(h + 1) * LANES]

    def bcast_rows(x, period, offset):
        return jnp.concatenate(
            [jnp.broadcast_to(x[g * period + offset:g * period + offset + 1], (period, x.shape[1]))
             for g in range(x.shape[0] // period)], axis=0)

    t_idx = _iota2(b.shape, 0)
    srow = _iota2((CHUNK, CHUNK), 0)
    scol = _iota2((CHUNK, CHUNK), 1)
    q_bf, k_bf = _bf(q), _bf(kk)
    q_lv, k_lv, sel = [_bf(q * f)], [k_bf], []
    size = 2
    while size <= CHUNK:
        half = size // 2
        if size == 4:
            pos = t_idx % size
            m = jnp.where(pos == 0, pltpu.roll(b, n - 1, axis=0),
                          jnp.where(pos == 1, b,
                                    jnp.where(pos == 2, pltpu.roll(b, 1, axis=0), pltpu.roll(b, 2, axis=0))))
        elif size > 4:
            m = bcast_rows(b, size, half - 1)
        if size > 2:
            ez = _bf(jnp.exp2(-jnp.abs(b - m)))
            q_lv.append(q_bf * ez)
            k_lv.append(k_bf * ez)
        sel.append((srow // size == scol // size) & (srow % size >= half) & (scol % size < half))
        size *= 2
    qe_all = q * jnp.exp2(b)
    kl_all = kk * jnp.exp2(bcast_rows(b, CHUNK, CHUNK - 1) - b)

    b_last = [blk(b, h, c)[CHUNK - 1:CHUNK] for h, c in probs]
    qe = [blk(qe_all, h, c) for h, c in probs]
    vc = [blk(v, h, c) for h, c in probs]

    s_lv = [[_dot_nt(blk(ql, h, c), blk(kl, h, c)) for h, c in probs] for ql, kl in zip(q_lv, k_lv)]
    a = []
    for j in range(len(probs)):
        acc = jnp.zeros((CHUNK, CHUNK), F32)
        for lv in range(len(sel)):
            acc = jnp.where(sel[lv], s_lv[lv][j], acc)
        a.append(acc)
    u = [_dot_tn(x, blk(kl_all, h, c)) for x, (h, c) in zip(vc, probs)]
    qk = q * kk
    o_intra = [_dot(x, y) + jnp.sum(blk(qk, h, c), axis=-1, keepdims=True) * y
               for x, y, (h, c) in zip(a, vc, probs)]

    st = list(st)
    outs = [[None] * len(chunks) for _ in heads]
    for c in chunks:
        for h in heads:
            j = h * len(chunks) + c
            outs[h][c] = o_intra[j] + _dot_nt(qe[j], st[h])
            st[h] = st[h] * jnp.exp2(b_last[j]) + u[j]

    cols = []
    for h in heads:
        o = jnp.concatenate(outs[h], axis=0)
        cols.append(o * lax.rsqrt(jnp.mean(o * o, axis=-1, keepdims=True) + NORM_EPS))
    o = jnp.concatenate(cols, axis=1)
    return _bf(o * norm_w * _silu(x[:, 3 * w:])), st


def _in_hgrn2_kernel(x_ref, nw_ref, w_ref, lbl_ref, hnw_ref, mu_ref, ohg_ref, rw_ref, st_ref, prev_ref):
    @pl.when(pl.program_id(1) == 0)
    def _():
        st_ref[...] = jnp.zeros_like(st_ref)
        prev_ref[...] = jnp.zeros_like(prev_ref)

    hb = _bf(_rmsnorm(x_ref[...], nw_ref[...]))
    rw = jnp.dot(hb, w_ref[:, HG_COLS:], preferred_element_type=F32)
    hg = jnp.dot(hb, w_ref[:, :HG_COLS], preferred_element_type=F32)
    rw_shift = _shift_rows(rw, prev_ref[...], 1)
    prev_ref[...] = rw[rw.shape[0] - SUBLANES:]
    rw_ref[...] = rw + (rw_shift - rw) * mu_ref[...]

    logits = lbl_ref[...]
    e = jnp.exp(logits - jnp.max(logits, axis=0, keepdims=True))
    lb = e[0:1] / jnp.sum(e, axis=0, keepdims=True)

    st = [st_ref[h] for h in range(HG_HEADS)]
    for i in range(ROW_TILE // HG_TILE):
        rows = slice(i * HG_TILE, (i + 1) * HG_TILE)
        ohg_ref[rows, :], st = _hgrn2_rows(hg[rows], lb, hnw_ref[...], st)
    for h in range(HG_HEADS):
        st_ref[h] = st[h]


def _in_hgrn2(x2, norm_w, w_in_bf, lb_logits, hg_norm_w, rw_mu, batch, seq):
    m = x2.shape[0]
    nt = seq // ROW_TILE

    def rows(width):
        return pl.BlockSpec((ROW_TILE, width), lambda b, t: (b * nt + t, 0))

    def whole(shape):
        return pl.BlockSpec(shape, lambda b, t: (0, 0), pipeline_mode=pl.Buffered(1))

    return pl.pallas_call(
        _in_hgrn2_kernel,
        grid=(batch, nt),
        in_specs=[rows(D_MODEL), whole((1, D_MODEL)), whole((D_MODEL, IN_COLS)),
                  whole((2, HG_WIDTH)), whole((1, HG_WIDTH)), whole((1, RW_COLS))],
        out_specs=[rows(HG_WIDTH), rows(RW_COLS)],
        out_shape=[
            jax.ShapeDtypeStruct((m, HG_WIDTH), BF16),
            jax.ShapeDtypeStruct((m, RW_COLS), F32),
        ],
        scratch_shapes=[pltpu.VMEM((HG_HEADS, HG_HEAD_DIM, HG_HEAD_DIM), F32),
                        pltpu.VMEM((SUBLANES, RW_COLS), F32)],
        compiler_params=pltpu.CompilerParams(
            dimension_semantics=("arbitrary", "arbitrary"), vmem_limit_bytes=VMEM_LIMIT_BYTES),
        name="in_hgrn2",
    )(x2, norm_w, w_in_bf, lb_logits, hg_norm_w, rw_mu)


def _stack2(x):
    lane = _iota2(x.shape, 1)
    return jnp.concatenate([jnp.where(lane < RW_HEAD_DIM, x, 0.0),
                            jnp.where(lane >= RW_HEAD_DIM, x, 0.0)], axis=0)


def _rwkv7_kernel(x_ref, w0_ref, wwa_ref, a0_ref, g2_ref, kk_ref, ka_ref, rk_ref, lnw_ref, lnb_ref,
                  o_ref, z_ref):
    @pl.when(pl.program_id(1) == 0)
    def _():
        z_ref[...] = jnp.zeros_like(z_ref)

    w = RW_WIDTH
    lane = _iota2((LANES, LANES), 1)
    row = _iota2((LANES, LANES), 0)
    same_head = (row // RW_HEAD_DIM) == (lane // RW_HEAD_DIM)
    head_ones = jnp.where(same_head, 1.0, 0.0).astype(F32)

    def head_sums(t, dot=_dot):
        return jnp.concatenate(
            [dot(t[:, p * LANES:(p + 1) * LANES], head_ones) for p in range(RW_PAIRS)], axis=1)

    wrow = _iota2((CHUNK, LANES), 0)
    wcol = _iota2((CHUNK, LANES), 1) % CHUNK
    strict = wcol < wrow
    incl = wcol <= wrow
    eye_w = jnp.where(wcol == wrow, 1.0, 0.0).astype(F32)
    eye = lane == row
    pairs = range(RW_PAIRS)

    def gates(rows):
        r, k, v = x_ref[rows, :w], x_ref[rows, w:2 * w], x_ref[rows, 2 * w:3 * w]
        wa = x_ref[rows, 3 * w:3 * w + LANES]
        gd = x_ref[rows, 3 * w + LANES:]
        n = r.shape[0]

        wa = jnp.where(_iota2(wa.shape, 1) < RW_DECAY_LORA, jnp.tanh(wa), wa)
        lora = _dot(wa, wwa_ref[...])
        logw = -RW_DECAY_SCALE * jax.nn.sigmoid(w0_ref[...] + lora[:, :w])
        a = jax.nn.sigmoid(a0_ref[...] + lora[:, w:])
        g = _dot(jax.nn.sigmoid(gd), g2_ref[...])

        kk = k * kk_ref[...]
        kk = kk / jnp.maximum(jnp.sqrt(head_sums(kk * kk)), L2_EPS)
        k = k * (1.0 + (a - 1.0) * ka_ref[...])
        beta = kk * a
        bonus = head_sums(r * k * rk_ref[...]) * v

        cum = _dot_exact_rhs(_chunk_tril(n), logw)
        e_neg = jnp.exp(-cum)
        return dict(n=n, cum=cum, a_t=-kk * jnp.exp(cum - logw), r_t=r * jnp.exp(cum), b_t=beta * e_neg,
                    k_t=k * e_neg, beta=beta, k=k, v=v, bonus=bonus, g=g)

    def recur(d, z):
        chunks = range(d["n"] // CHUNK)
        probs = [(p, c) for p in pairs for c in chunks]

        def blk(t, p, c):
            return t[c * CHUNK:(c + 1) * CHUNK, p * LANES:(p + 1) * LANES]

        c_last = [blk(d["cum"], p, c)[CHUNK - 1:CHUNK] for p, c in probs]
        e_last = [jnp.exp(cl - blk(d["cum"], p, c)) for (p, c), cl in zip(probs, c_last)]
        a_c = [blk(d["a_t"], p, c) for p, c in probs]
        r_c = [blk(d["r_t"], p, c) for p, c in probs]
        b_h = [blk(d["beta"], p, c) * e for (p, c), e in zip(probs, e_last)]
        k_h = [blk(d["k"], p, c) * e for (p, c), e in zip(probs, e_last)]
        vc = [blk(d["v"], p, c) for p, c in probs]
        v_st = [_stack2(t) for t in vc]

        s4 = [_dot_nt(jnp.concatenate([aa, rr], axis=0),
                      jnp.concatenate([_stack2(blk(d["b_t"], p, c)), _stack2(blk(d["k_t"], p, c))], axis=0))
              for aa, rr, (p, c) in zip(a_c, r_c, probs)]
        l_ab = [jnp.where(strict, s[:CHUNK, :LANES], 0.0) for s in s4]
        l_ak = [jnp.where(strict, s[:CHUNK, LANES:], 0.0) for s in s4]
        a_rb = [jnp.where(incl, s[CHUNK:, :LANES], 0.0) for s in s4]
        a_rk = [jnp.where(incl, s[CHUNK:, LANES:], 0.0) for s in s4]

        tm = [l + eye_w for l in l_ab]
        lp = [_dot(l, _stack2(l)) for l in l_ab]
        w1 = [_dot(l, t) for l, t in zip(l_ak, v_st)]
        for _ in range(4):
            both = [_dot(jnp.concatenate([t, l], axis=0), _stack2(l)) for t, l in zip(tm, lp)]
            tm = [t + bo[:CHUNK] for t, bo in zip(tm, both)]
            lp = [bo[CHUNK:] for bo in both]
        tm = [t + _dot(t, _stack2(l)) for t, l in zip(tm, lp)]

        mp = [_dot(t, jnp.concatenate([_stack2(aa), _stack2(ww)], axis=1))
              for t, aa, ww in zip(tm, a_c, w1)]
        ry = [_dot(jnp.concatenate([lb, lk], axis=1),
                   jnp.concatenate([jnp.concatenate([_stack2(t[:, :LANES]), _stack2(t[:, LANES:])], axis=1),
                                    jnp.concatenate([jnp.zeros_like(vs), vs], axis=1)], axis=0))
              for lb, lk, t, vs in zip(a_rb, a_rk, mp, v_st)]
        rq = [rr + t[:, :LANES] for rr, t in zip(r_c, ry)]
        y0 = [t[:, LANES:] for t in ry]
        gu = [_dot_tn(jnp.concatenate([bh, kh], axis=0),
                      jnp.concatenate([t, jnp.concatenate([jnp.zeros_like(vv), vv], axis=1)], axis=0))
              for bh, kh, t, vv in zip(b_h, k_h, mp, vc)]
        g_mat = [jnp.where(same_head, t[:, :LANES], 0.0) + jnp.where(eye, jnp.exp(cl), 0.0)
                 for t, cl in zip(gu, c_last)]
        u_mat = [jnp.where(same_head, t[:, LANES:], 0.0) for t in gu]

        z = list(z)
        ys = [[None] * len(chunks) for _ in pairs]
        for c in chunks:
            for p in pairs:
                j = p * len(chunks) + c
                out = _dot(jnp.concatenate([g_mat[j], rq[j]], axis=0), z[p])
                ys[p][c] = out[LANES:] + y0[j]
                z[p] = out[:LANES] + u_mat[j]
        return jnp.concatenate([jnp.concatenate(ys[p], axis=0) for p in pairs], axis=1), z

    def finish(y, d):
        inv_n = 1.0 / RW_HEAD_DIM
        mean = head_sums(y, _dot_hilo_lhs) * inv_n
        yc = y - mean
        var = head_sums(yc * yc) * inv_n
        y = yc * lax.rsqrt(var + RW_GN_EPS) * lnw_ref[...] + lnb_ref[...]
        return _bf((y + d["bonus"]) * d["g"])

    z = [z_ref[p] for p in pairs]
    pending = None
    for i in range(x_ref.shape[0] // RW_BLOCK):
        rows = slice(i * RW_BLOCK, (i + 1) * RW_BLOCK)
        d = gates(rows)
        y, z = recur(d, z)
        if pending is not None:
            o_ref[pending[0], :] = finish(*pending[1:])
        pending = (rows, y, d)
    o_ref[pending[0], :] = finish(*pending[1:])
    for p in pairs:
        z_ref[p] = z[p]


def _rwkv7(rw_p, w0, wwa_bf, a0, g2_bf, k_k, k_a, r_k, ln_w, ln_b, batch, seq):
    m = rw_p.shape[0]
    nt = seq // SCAN_TILE

    def vec(width):
        return pl.BlockSpec((1, width), lambda b, t: (0, 0))

    return pl.pallas_call(
        _rwkv7_kernel,
        grid=(batch, nt),
        in_specs=[
            pl.BlockSpec((SCAN_TILE, RW_COLS), lambda b, t: (b * nt + t, 0)),
            vec(RW_WIDTH),
            pl.BlockSpec((LANES, 2 * RW_WIDTH), lambda b, t: (0, 0)),
            vec(RW_WIDTH),
            pl.BlockSpec((RW_GATE_LORA, RW_WIDTH), lambda b, t: (0, 0)),
            vec(RW_WIDTH), vec(RW_WIDTH), vec(RW_WIDTH), vec(RW_WIDTH), vec(RW_WIDTH),
        ],
        out_specs=pl.BlockSpec((SCAN_TILE, RW_WIDTH), lambda b, t: (b * nt + t, 0)),
        out_shape=jax.ShapeDtypeStruct((m, RW_WIDTH), BF16),
        scratch_shapes=[pltpu.VMEM((RW_PAIRS, LANES, LANES), F32)],
        compiler_params=pltpu.CompilerParams(
            dimension_semantics=("arbitrary", "arbitrary"), vmem_limit_bytes=VMEM_LIMIT_BYTES),
        name="rwkv7",
    )(rw_p, w0, wwa_bf, a0, g2_bf, k_k, k_a, r_k, ln_w, ln_b)


def _mix_ffn_kernel(x_ref, ohg_ref, orw_ref, wo_ref, n2_ref, wu_ref, cw_ref, cb_ref, wd_ref, nf_ref,
                    o_ref, prev_ref, u0_ref, u1_ref, act_ref):
    u_refs = (u0_ref, u1_ref)
    hist = SUBLANES

    @pl.when(pl.program_id(1) == 0)
    def _():
        prev_ref[...] = jnp.zeros_like(prev_ref)

    x1 = (x_ref[...]
          + jnp.dot(ohg_ref[...], wo_ref[:HG_WIDTH], preferred_element_type=F32)
          + jnp.dot(orw_ref[...], wo_ref[HG_WIDTH:], preferred_element_type=F32))
    h = _bf(_rmsnorm(x1, n2_ref[...]))
    n = h.shape[0]

    ct = FFN_COL_TILE
    ntile = D_FF // ct

    def halves(j):
        return ((slice(0, ct), slice(j * ct, (j + 1) * ct)),
                (slice(ct, 2 * ct), slice(D_FF + j * ct, D_FF + (j + 1) * ct)))

    def up(j):
        u = u_refs[j % 2]
        for dst, cols in halves(j):
            u[0:hist, dst] = prev_ref[:, cols]
            u[hist:, dst] = jnp.dot(h, wu_ref[:, cols], preferred_element_type=F32)
            prev_ref[:, cols] = u[n:n + hist, dst]

    def conv_gate(j):
        u = u_refs[j % 2]
        c = []
        for dst, cols in halves(j):
            cw = cw_ref[:, cols]
            c.append(cw[2:3] * u[hist:n + hist, dst] + cw[1:2] * u[hist - 1:n + hist - 1, dst]
                     + cw[0:1] * u[hist - 2:n + hist - 2, dst] + cb_ref[:, cols])
        act_ref[:, j * ct:(j + 1) * ct] = _bf(_silu(c[0]) * c[1])

    up(0)
    for j in range(ntile):
        if j + 1 < ntile:
            up(j + 1)
        conv_gate(j)
    y = x1 + jnp.dot(act_ref[...], wd_ref[...], preferred_element_type=F32)
    o_ref[...] = _rmsnorm(y, nf_ref[...])


def _mix_ffn(x2, o_hg, o_rw, w_out_bf, norm2_w, w_up_bf, conv_w, conv_b, w_down_bf, final_w, batch, seq):
    m = x2.shape[0]
    nt = seq // ROW_TILE

    def rows(width):
        return pl.BlockSpec((ROW_TILE, width), lambda b, t: (b * nt + t, 0))

    def whole(shape):
        return pl.BlockSpec(shape, lambda b, t: (0, 0), pipeline_mode=pl.Buffered(1))

    return pl.pallas_call(
        _mix_ffn_kernel,
        grid=(batch, nt),
        in_specs=[
            rows(D_MODEL), rows(HG_WIDTH), rows(RW_WIDTH),
            whole((D_MODEL, D_MODEL)), whole((1, D_MODEL)),
            whole((D_MODEL, 2 * D_FF)), whole((3, 2 * D_FF)), whole((1, 2 * D_FF)),
            whole((D_FF, D_MODEL)), whole((1, D_MODEL)),
        ],
        out_specs=rows(D_MODEL),
        out_shape=jax.ShapeDtypeStruct((m, D_MODEL), F32),
        scratch_shapes=[pltpu.VMEM((SUBLANES, 2 * D_FF), F32),
                        pltpu.VMEM((SUBLANES + ROW_TILE, 2 * FFN_COL_TILE), F32),
                        pltpu.VMEM((SUBLANES + ROW_TILE, 2 * FFN_COL_TILE), F32),
                        pltpu.VMEM((ROW_TILE, D_FF), BF16)],
        compiler_params=pltpu.CompilerParams(
            dimension_semantics=("arbitrary", "arbitrary"), vmem_limit_bytes=VMEM_LIMIT_BYTES),
        name="mix_ffn",
    )(x2, o_hg, o_rw, w_out_bf, norm2_w, w_up_bf, conv_w, conv_b, w_down_bf, final_w)


def kernel(x, norm1_w, w_in, hg_lb_logits, hg_norm_w, rw_shift_mu, rw_w0, rw_w2, rw_a0, rw_a2, rw_g2,
           rw_k_k, rw_k_a, rw_r_k, rw_ln_w, rw_ln_b, w_out, norm2_w, w_up, conv_w, conv_b, w_down,
           final_norm_w):
    batch, seq, _ = x.shape
    assert seq % SCAN_TILE == 0 and seq % ROW_TILE == 0 and norm1_w.shape[0] == 1
    assert CHUNK == RW_HEAD_DIM and 2 * RW_HEAD_DIM == LANES
    x2 = x.reshape(batch * seq, D_MODEL)

    zeros = jnp.zeros_like(rw_w2[0])
    wwa = _bf(jnp.concatenate([jnp.concatenate([rw_w2[0], zeros], axis=1),
                               jnp.concatenate([zeros, rw_a2[0]], axis=1)], axis=0))

    o_hg, rw_p = _in_hgrn2(x2, norm1_w, _bf(w_in[0]), hg_lb_logits, hg_norm_w, rw_shift_mu, batch, seq)
    o_rw = _rwkv7(rw_p, rw_w0, wwa, rw_a0, _bf(rw_g2[0]), rw_k_k, rw_k_a, rw_r_k, rw_ln_w, rw_ln_b, batch, seq)
    out = _mix_ffn(x2, o_hg, o_rw, _bf(w_out[0]), norm2_w, _bf(w_up[0]), conv_w[0], conv_b, _bf(w_down[0]),
                   final_norm_w.reshape(1, D_MODEL), batch, seq)
    return out.reshape(batch, seq, D_MODEL)
```

```python
import jax
import jax.numpy as jnp
from jax import lax
from jax.experimental import pallas as pl
from jax.experimental.pallas import tpu as pltpu

F32 = jnp.float32
BF16 = jnp.bfloat16

D_MODEL = 1024
HG_WIDTH = 512
HG_HEAD_DIM = 128
HG_HEADS = HG_WIDTH // HG_HEAD_DIM
RW_WIDTH = 512
RW_HEAD_DIM = 64
RW_DECAY_LORA = 64
RW_AAA_LORA = 64
RW_GATE_LORA = 128
RW_COLS = 3 * RW_WIDTH + RW_DECAY_LORA + RW_AAA_LORA + RW_GATE_LORA
HG_COLS = 4 * HG_WIDTH
IN_COLS = HG_COLS + RW_COLS
D_FF = 2816
NORM_EPS = 1e-6
RW_GN_EPS = 64e-5
L2_EPS = 1e-12
RW_DECAY_SCALE = 0.6065306597126334

LANES = 128
SUBLANES = 8
CHUNK = 64
RW_PAIRS = RW_WIDTH // LANES
VMEM_LIMIT_BYTES = 56 * 1024 * 1024

ROW_TILE = 512
SCAN_TILE = 512
RW_BLOCK = 256
HG_TILE = 256
FFN_COL_TILE = 256
CONV_ROWS = 128


def _bf(x):
    return x.astype(BF16)


def _dot(a, b):
    return jnp.dot(_bf(a), _bf(b), preferred_element_type=F32)


def _dot_nt(a, b):
    return lax.dot_general(_bf(a), _bf(b), (((1,), (1,)), ((), ())), preferred_element_type=F32)


def _dot_tn(a, b):
    return jnp.dot(_bf(a.T), _bf(b), preferred_element_type=F32)


def _split3(x):
    h1 = _bf(x)
    r1 = x - h1.astype(F32)
    h2 = _bf(r1)
    r2 = r1 - h2.astype(F32)
    return h1, h2, _bf(r2)


def _dot_exact_rhs(m01, x):
    w = x.shape[1]
    out = jnp.dot(_bf(m01), jnp.concatenate(_split3(x), axis=1), preferred_element_type=F32)
    return out[:, :w] + out[:, w:2 * w] + out[:, 2 * w:]


def _dot_hilo_lhs(x, m01):
    n = x.shape[0]
    hi = _bf(x)
    lo = _bf(x - hi.astype(F32))
    out = jnp.dot(jnp.concatenate([hi, lo], axis=0), _bf(m01), preferred_element_type=F32)
    return out[:n] + out[n:]


def _iota2(shape, dim):
    return lax.broadcasted_iota(jnp.int32, shape, dim)


def _chunk_tril(n):
    r = _iota2((n, n), 0)
    c = _iota2((n, n), 1)
    return jnp.where((r // CHUNK == c // CHUNK) & (c <= r), 1.0, 0.0).astype(F32)


def _shift_rows(x, prev_rows, k):
    rolled = pltpu.roll(x, k, axis=0)
    prev = pltpu.roll(prev_rows, k, axis=0)
    head = jnp.where(_iota2(prev.shape, 0) < k, prev, rolled[:SUBLANES])
    return jnp.concatenate([head, rolled[SUBLANES:]], axis=0)


def _silu(x):
    return x * jax.nn.sigmoid(x)


def _rmsnorm(x, w):
    return x * lax.rsqrt(jnp.mean(x * x, axis=-1, keepdims=True) + NORM_EPS) * w


def _hgrn2_rows(x, lb, norm_w, st):
    w = HG_WIDTH
    f = lb + (1.0 - lb) * jax.nn.sigmoid(x[:, w:2 * w])
    kk = 1.0 - f
    q = _silu(x[:, :w]) * (HG_HEAD_DIM ** -0.5)
    v = x[:, 2 * w:3 * w]
    n = q.shape[0]
    b = _dot_exact_rhs(_chunk_tril(n), jnp.log2(f))

    heads = range(HG_HEADS)
    chunks = range(n // CHUNK)
    probs = [(h, c) for h in heads for c in chunks]

    def blk(x, h, c):
        return x[c * CHUNK:(c + 1) * CHUNK, h * LANES:(h + 1) * LANES]

    def bcast_rows(x, period, offset):
        return jnp.concatenate(
            [jnp.broadcast_to(x[g * period + offset:g * period + offset + 1], (period, x.shape[1]))
             for g in range(x.shape[0] // period)], axis=0)

    t_idx = _iota2(b.shape, 0)
    srow = _iota2((CHUNK, CHUNK), 0)
    scol = _iota2((CHUNK, CHUNK), 1)
    q_bf, k_bf = _bf(q), _bf(kk)
    q_lv, k_lv, sel = [_bf(q * f)], [k_bf], []
    size = 2
    while size <= CHUNK:
        half = size // 2
        if size == 4:
            pos = t_idx % size
            m = jnp.where(pos == 0, pltpu.roll(b, n - 1, axis=0),
                          jnp.where(pos == 1, b,
                                    jnp.where(pos == 2, pltpu.roll(b, 1, axis=0), pltpu.roll(b, 2, axis=0))))
        elif size > 4:
            m = bcast_rows(b, size, half - 1)
        if size > 2:
            ez = _bf(jnp.exp2(-jnp.abs(b - m)))
            q_lv.append(q_bf * ez)
            k_lv.append(k_bf * ez)
        sel.append((srow // size == scol // size) & (srow % size >= half) & (scol % size < half))
        size *= 2
    qe_all = q * jnp.exp2(b)
    kl_all = kk * jnp.exp2(bcast_rows(b, CHUNK, CHUNK - 1) - b)

    b_last = [blk(b, h, c)[CHUNK - 1:CHUNK] for h, c in probs]
    qe = [blk(qe_all, h, c) for h, c in probs]
    vc = [blk(v, h, c) for h, c in probs]

    s_lv = [[_dot_nt(blk(ql, h, c), blk(kl, h, c)) for h, c in probs] for ql, kl in zip(q_lv, k_lv)]
    a = []
    for j in range(len(probs)):
        acc = jnp.zeros((CHUNK, CHUNK), F32)
        for lv in range(len(sel)):
            acc = jnp.where(sel[lv], s_lv[lv][j], acc)
        a.append(acc)
    u = [_dot_tn(x, blk(kl_all, h, c)) for x, (h, c) in zip(vc, probs)]
    qk = q * kk
    o_intra = [_dot(x, y) + jnp.sum(blk(qk, h, c), axis=-1, keepdims=True) * y
               for x, y, (h, c) in zip(a, vc, probs)]

    st = list(st)
    outs = [[None] * len(chunks) for _ in heads]
    for c in chunks:
        for h in heads:
            j = h * len(chunks) + c
            outs[h][c] = o_intra[j] + _dot_nt(qe[j], st[h])
            st[h] = st[h] * jnp.exp2(b_last[j]) + u[j]

    cols = []
    for h in heads:
        o = jnp.concatenate(outs[h], axis=0)
        cols.append(o * lax.rsqrt(jnp.mean(o * o, axis=-1, keepdims=True) + NORM_EPS))
    o = jnp.concatenate(cols, axis=1)
    return _bf(o * norm_w * _silu(x[:, 3 * w:])), st


def _in_hgrn2_kernel(x_ref, nw_ref, w_ref, lbl_ref, hnw_ref, mu_ref, ohg_ref, rw_ref, st_ref, prev_ref):
    @pl.when(pl.program_id(1) == 0)
    def _():
        st_ref[...] = jnp.zeros_like(st_ref)
        prev_ref[...] = jnp.zeros_like(prev_ref)

    hb = _bf(_rmsnorm(x_ref[...], nw_ref[...]))
    rw = jnp.dot(hb, w_ref[:, HG_COLS:], preferred_element_type=F32)
    hg = jnp.dot(hb, w_ref[:, :HG_COLS], preferred_element_type=F32)
    rw_shift = _shift_rows(rw, prev_ref[...], 1)
    prev_ref[...] = rw[rw.shape[0] - SUBLANES:]
    rw_ref[...] = rw + (rw_shift - rw) * mu_ref[...]

    logits = lbl_ref[...]
    e = jnp.exp(logits - jnp.max(logits, axis=0, keepdims=True))
    lb = e[0:1] / jnp.sum(e, axis=0, keepdims=True)

    st = [st_ref[h] for h in range(HG_HEADS)]
    for i in range(ROW_TILE // HG_TILE):
        rows = slice(i * HG_TILE, (i + 1) * HG_TILE)
        ohg_ref[rows, :], st = _hgrn2_rows(hg[rows], lb, hnw_ref[...], st)
    for h in range(HG_HEADS):
        st_ref[h] = st[h]


def _in_hgrn2(x2, norm_w, w_in_bf, lb_logits, hg_norm_w, rw_mu, batch, seq):
    m = x2.shape[0]
    nt = seq // ROW_TILE

    def rows(width):
        return pl.BlockSpec((ROW_TILE, width), lambda b, t: (b * nt + t, 0))

    def whole(shape):
        return pl.BlockSpec(shape, lambda b, t: (0, 0), pipeline_mode=pl.Buffered(1))

    return pl.pallas_call(
        _in_hgrn2_kernel,
        grid=(batch, nt),
        in_specs=[rows(D_MODEL), whole((1, D_MODEL)), whole((D_MODEL, IN_COLS)),
                  whole((2, HG_WIDTH)), whole((1, HG_WIDTH)), whole((1, RW_COLS))],
        out_specs=[rows(HG_WIDTH), rows(RW_COLS)],
        out_shape=[
            jax.ShapeDtypeStruct((m, HG_WIDTH), BF16),
            jax.ShapeDtypeStruct((m, RW_COLS), F32),
        ],
        scratch_shapes=[pltpu.VMEM((HG_HEADS, HG_HEAD_DIM, HG_HEAD_DIM), F32),
                        pltpu.VMEM((SUBLANES, RW_COLS), F32)],
        compiler_params=pltpu.CompilerParams(
            dimension_semantics=("arbitrary", "arbitrary"), vmem_limit_bytes=VMEM_LIMIT_BYTES),
        name="in_hgrn2",
    )(x2, norm_w, w_in_bf, lb_logits, hg_norm_w, rw_mu)


def _stack2(x):
    lane = _iota2(x.shape, 1)
    return jnp.concatenate([jnp.where(lane < RW_HEAD_DIM, x, 0.0),
                            jnp.where(lane >= RW_HEAD_DIM, x, 0.0)], axis=0)


def _rwkv7_kernel(x_ref, w0_ref, wwa_ref, a0_ref, g2_ref, kk_ref, ka_ref, rk_ref, lnw_ref, lnb_ref,
                  o_ref, z_ref):
    @pl.when(pl.program_id(1) == 0)
    def _():
        z_ref[...] = jnp.zeros_like(z_ref)

    w = RW_WIDTH
    lane = _iota2((LANES, LANES), 1)
    row = _iota2((LANES, LANES), 0)
    same_head = (row // RW_HEAD_DIM) == (lane // RW_HEAD_DIM)
    head_ones = jnp.where(same_head, 1.0, 0.0).astype(F32)

    def head_sums(t, dot=_dot):
        return jnp.concatenate(
            [dot(t[:, p * LANES:(p + 1) * LANES], head_ones) for p in range(RW_PAIRS)], axis=1)

    wrow = _iota2((CHUNK, LANES), 0)
    wcol = _iota2((CHUNK, LANES), 1) % CHUNK
    strict = wcol < wrow
    incl = wcol <= wrow
    eye_w = jnp.where(wcol == wrow, 1.0, 0.0).astype(F32)
    eye = lane == row
    pairs = range(RW_PAIRS)

    def gates(rows):
        r, k, v = x_ref[rows, :w], x_ref[rows, w:2 * w], x_ref[rows, 2 * w:3 * w]
        wa = x_ref[rows, 3 * w:3 * w + LANES]
        gd = x_ref[rows, 3 * w + LANES:]
        n = r.shape[0]

        wa = jnp.where(_iota2(wa.shape, 1) < RW_DECAY_LORA, jnp.tanh(wa), wa)
        lora = _dot(wa, wwa_ref[...])
        logw = -RW_DECAY_SCALE * jax.nn.sigmoid(w0_ref[...] + lora[:, :w])
        a = jax.nn.sigmoid(a0_ref[...] + lora[:, w:])
        g = _dot(jax.nn.sigmoid(gd), g2_ref[...])

        kk = k * kk_ref[...]
        kk = kk / jnp.maximum(jnp.sqrt(head_sums(kk * kk)), L2_EPS)
        k = k * (1.0 + (a - 1.0) * ka_ref[...])
        beta = kk * a
        bonus = head_sums(r * k * rk_ref[...]) * v

        cum = _dot_exact_rhs(_chunk_tril(n), logw)
        e_neg = jnp.exp(-cum)
        return dict(n=n, cum=cum, a_t=-kk * jnp.exp(cum - logw), r_t=r * jnp.exp(cum), b_t=beta * e_neg,
                    k_t=k * e_neg, beta=beta, k=k, v=v, bonus=bonus, g=g)

    def recur(d, z):
        chunks = range(d["n"] // CHUNK)
        probs = [(p, c) for p in pairs for c in chunks]

        def blk(t, p, c):
            return t[c * CHUNK:(c + 1) * CHUNK, p * LANES:(p + 1) * LANES]

        c_last = [blk(d["cum"], p, c)[CHUNK - 1:CHUNK] for p, c in probs]
        e_last = [jnp.exp(cl - blk(d["cum"], p, c)) for (p, c), cl in zip(probs, c_last)]
        a_c = [blk(d["a_t"], p, c) for p, c in probs]
        r_c = [blk(d["r_t"], p, c) for p, c in probs]
        b_h = [blk(d["beta"], p, c) * e for (p, c), e in zip(probs, e_last)]
        k_h = [blk(d["k"], p, c) * e for (p, c), e in zip(probs, e_last)]
        vc = [blk(d["v"], p, c) for p, c in probs]
        v_st = [_stack2(t) for t in vc]

        s4 = [_dot_nt(jnp.concatenate([aa, rr], axis=0),
                      jnp.concatenate([_stack2(blk(d["b_t"], p, c)), _stack2(blk(d["k_t"], p, c))], axis=0))
              for aa, rr, (p, c) in zip(a_c, r_c, probs)]
        l_ab = [jnp.where(strict, s[:CHUNK, :LANES], 0.0) for s in s4]
        l_ak = [jnp.where(strict, s[:CHUNK, LANES:], 0.0) for s in s4]
        a_rb = [jnp.where(incl, s[CHUNK:, :LANES], 0.0) for s in s4]
        a_rk = [jnp.where(incl, s[CHUNK:, LANES:], 0.0) for s in s4]

        tm = [l + eye_w for l in l_ab]
        lp = [_dot(l, _stack2(l)) for l in l_ab]
        w1 = [_dot(l, t) for l, t in zip(l_ak, v_st)]
        for _ in range(4):
            both = [_dot(jnp.concatenate([t, l], axis=0), _stack2(l)) for t, l in zip(tm, lp)]
            tm = [t + bo[:CHUNK] for t, bo in zip(tm, both)]
            lp = [bo[CHUNK:] for bo in both]
        tm = [t + _dot(t, _stack2(l)) for t, l in zip(tm, lp)]

        mp = [_dot(t, jnp.concatenate([_stack2(aa), _stack2(ww)], axis=1))
              for t, aa, ww in zip(tm, a_c, w1)]
        ry = [_dot(jnp.concatenate([lb, lk], axis=1),
                   jnp.concatenate([jnp.concatenate([_stack2(t[:, :LANES]), _stack2(t[:, LANES:])], axis=1),
                                    jnp.concatenate([jnp.zeros_like(vs), vs], axis=1)], axis=0))
              for lb, lk, t, vs in zip(a_rb, a_rk, mp, v_st)]
        rq = [rr + t[:, :LANES] for rr, t in zip(r_c, ry)]
        y0 = [t[:, LANES:] for t in ry]
        gu = [_dot_tn(jnp.concatenate([bh, kh], axis=0),
                      jnp.concatenate([t, jnp.concatenate([jnp.zeros_like(vv), vv], axis=1)], axis=0))
              for bh, kh, t, vv in zip(b_h, k_h, mp, vc)]
        g_mat = [jnp.where(same_head, t[:, :LANES], 0.0) + jnp.where(eye, jnp.exp(cl), 0.0)
                 for t, cl in zip(gu, c_last)]
        u_mat = [jnp.where(same_head, t[:, LANES:], 0.0) for t in gu]

        z = list(z)
        ys = [[None] * len(chunks) for _ in pairs]
        for c in chunks:
            for p in pairs:
                j = p * len(chunks) + c
                out = _dot(jnp.concatenate([g_mat[j], rq[j]], axis=0), z[p])
                ys[p][c] = out[LANES:] + y0[j]
                z[p] = out[:LANES] + u_mat[j]
        return jnp.concatenate([jnp.concatenate(ys[p], axis=0) for p in pairs], axis=1), z

    def finish(y, d):
        inv_n = 1.0 / RW_HEAD_DIM
        mean = head_sums(y, _dot_hilo_lhs) * inv_n
        yc = y - mean
        var = head_sums(yc * yc) * inv_n
        y = yc * lax.rsqrt(var + RW_GN_EPS) * lnw_ref[...] + lnb_ref[...]
        return _bf((y + d["bonus"]) * d["g"])

    z = [z_ref[p] for p in pairs]
    pending = None
    for i in range(x_ref.shape[0] // RW_BLOCK):
        rows = slice(i * RW_BLOCK, (i + 1) * RW_BLOCK)
        d = gates(rows)
        y, z = recur(d, z)
        if pending is not None:
            o_ref[pending[0], :] = finish(*pending[1:])
        pending = (rows, y, d)
    o_ref[pending[0], :] = finish(*pending[1:])
    for p in pairs:
        z_ref[p] = z[p]


def _rwkv7(rw_p, w0, wwa_bf, a0, g2_bf, k_k, k_a, r_k, ln_w, ln_b, batch, seq):
    m = rw_p.shape[0]
    nt = seq // SCAN_TILE

    def vec(width):
        return pl.BlockSpec((1, width), lambda b, t: (0, 0))

    return pl.pallas_call(
        _rwkv7_kernel,
        grid=(batch, nt),
        in_specs=[
            pl.BlockSpec((SCAN_TILE, RW_COLS), lambda b, t: (b * nt + t, 0)),
            vec(RW_WIDTH),
            pl.BlockSpec((LANES, 2 * RW_WIDTH), lambda b, t: (0, 0)),
            vec(RW_WIDTH),
            pl.BlockSpec((RW_GATE_LORA, RW_WIDTH), lambda b, t: (0, 0)),
            vec(RW_WIDTH), vec(RW_WIDTH), vec(RW_WIDTH), vec(RW_WIDTH), vec(RW_WIDTH),
        ],
        out_specs=pl.BlockSpec((SCAN_TILE, RW_WIDTH), lambda b, t: (b * nt + t, 0)),
        out_shape=jax.ShapeDtypeStruct((m, RW_WIDTH), BF16),
        scratch_shapes=[pltpu.VMEM((RW_PAIRS, LANES, LANES), F32)],
        compiler_params=pltpu.CompilerParams(
            dimension_semantics=("arbitrary", "arbitrary"), vmem_limit_bytes=VMEM_LIMIT_BYTES),
        name="rwkv7",
    )(rw_p, w0, wwa_bf, a0, g2_bf, k_k, k_a, r_k, ln_w, ln_b)


def _mix_ffn_kernel(x_ref, ohg_ref, orw_ref, wo_ref, n2_ref, wu_ref, cw_ref, cb_ref, wd_ref, nf_ref,
                    o_ref, prev_ref, u0_ref, u1_ref, act_ref):
    u_refs = (u0_ref, u1_ref)
    hist = SUBLANES

    @pl.when(pl.program_id(1) == 0)
    def _():
        prev_ref[...] = jnp.zeros_like(prev_ref)

    x1 = (x_ref[...]
          + jnp.dot(ohg_ref[...], wo_ref[:HG_WIDTH], preferred_element_type=F32)
          + jnp.dot(orw_ref[...], wo_ref[HG_WIDTH:], preferred_element_type=F32))
    h = _bf(_rmsnorm(x1, n2_ref[...]))
    n = h.shape[0]

    ct = FFN_COL_TILE
    ntile = D_FF // ct

    def halves(j):
        return ((slice(0, ct), slice(j * ct, (j + 1) * ct)),
                (slice(ct, 2 * ct), slice(D_FF + j * ct, D_FF + (j + 1) * ct)))

    def up(j):
        u = u_refs[j % 2]
        for dst, cols in halves(j):
            u[0:hist, dst] = prev_ref[:, cols]
            u[hist:, dst] = jnp.dot(h, wu_ref[:, cols], preferred_element_type=F32)
            prev_ref[:, cols] = u[n:n + hist, dst]

    def conv_gate(j):
        u = u_refs[j % 2]
        for r0 in range(0, n, CONV_ROWS):
            c = []
            for dst, cols in halves(j):
                cw = cw_ref[:, cols]
                lo = r0 + hist
                c.append(cw[2:3] * u[lo:lo + CONV_ROWS, dst] + cw[1:2] * u[lo - 1:lo - 1 + CONV_ROWS, dst]
                         + cw[0:1] * u[lo - 2:lo - 2 + CONV_ROWS, dst] + cb_ref[:, cols])
            act_ref[r0:r0 + CONV_ROWS, j * ct:(j + 1) * ct] = _bf(_silu(c[0]) * c[1])

    up(0)
    for j in range(ntile):
        if j + 1 < ntile:
            up(j + 1)
        conv_gate(j)
    y = x1 + jnp.dot(act_ref[...], wd_ref[...], preferred_element_type=F32)
    o_ref[...] = _rmsnorm(y, nf_ref[...])


def _mix_ffn(x2, o_hg, o_rw, w_out_bf, norm2_w, w_up_bf, conv_w, conv_b, w_down_bf, final_w, batch, seq):
    m = x2.shape[0]
    nt = seq // ROW_TILE

    def rows(width):
        return pl.BlockSpec((ROW_TILE, width), lambda b, t: (b * nt + t, 0))

    def whole(shape):
        return pl.BlockSpec(shape, lambda b, t: (0, 0), pipeline_mode=pl.Buffered(1))

    return pl.pallas_call(
        _mix_ffn_kernel,
        grid=(batch, nt),
        in_specs=[
            rows(D_MODEL), rows(HG_WIDTH), rows(RW_WIDTH),
            whole((D_MODEL, D_MODEL)), whole((1, D_MODEL)),
            whole((D_MODEL, 2 * D_FF)), whole((3, 2 * D_FF)), whole((1, 2 * D_FF)),
            whole((D_FF, D_MODEL)), whole((1, D_MODEL)),
        ],
        out_specs=rows(D_MODEL),
        out_shape=jax.ShapeDtypeStruct((m, D_MODEL), F32),
        scratch_shapes=[pltpu.VMEM((SUBLANES, 2 * D_FF), F32),
                        pltpu.VMEM((SUBLANES + ROW_TILE, 2 * FFN_COL_TILE), F32),
                        pltpu.VMEM((SUBLANES + ROW_TILE, 2 * FFN_COL_TILE), F32),
                        pltpu.VMEM((ROW_TILE, D_FF), BF16)],
        compiler_params=pltpu.CompilerParams(
            dimension_semantics=("arbitrary", "arbitrary"), vmem_limit_bytes=VMEM_LIMIT_BYTES),
        name="mix_ffn",
    )(x2, o_hg, o_rw, w_out_bf, norm2_w, w_up_bf, conv_w, conv_b, w_down_bf, final_w)


def kernel(x, norm1_w, w_in, hg_lb_logits, hg_norm_w, rw_shift_mu, rw_w0, rw_w2, rw_a0, rw_a2, rw_g2,
           rw_k_k, rw_k_a, rw_r_k, rw_ln_w, rw_ln_b, w_out, norm2_w, w_up, conv_w, conv_b, w_down,
           final_norm_w):
    batch, seq, _ = x.shape
    assert seq % SCAN_TILE == 0 and seq % ROW_TILE == 0 and norm1_w.shape[0] == 1
    assert CHUNK == RW_HEAD_DIM and 2 * RW_HEAD_DIM == LANES
    x2 = x.reshape(batch * seq, D_MODEL)

    zeros = jnp.zeros_like(rw_w2[0])
    wwa = _bf(jnp.concatenate([jnp.concatenate([rw_w2[0], zeros], axis=1),
                               jnp.concatenate([zeros, rw_a2[0]], axis=1)], axis=0))

    o_hg, rw_p = _in_hgrn2(x2, norm1_w, _bf(w_in[0]), hg_lb_logits, hg_norm_w, rw_shift_mu, batch, seq)
    o_rw = _rwkv7(rw_p, rw_w0, wwa, rw_a0, _bf(rw_g2[0]), rw_k_k, rw_k_a, rw_r_k, rw_ln_w, rw_ln_b, batch, seq)
    out = _mix_ffn(x2, o_hg, o_rw, _bf(w_out[0]), norm2_w, _bf(w_up[0]), conv_w[0], conv_b, _bf(w_down[0]),
                   final_norm_w.reshape(1, D_MODEL), batch, seq)
    return out.reshape(batch, seq, D_MODEL)
```
